```python
import math
import jax, jax.numpy as jnp
from jax import lax
import numpy as np

D_MODEL = 1024
BATCH = 8
SEQ = 2048
DEPTH = 4
DEC_BATCH = 32
DEC_SEQ = 8
PAST_LEN = 8192
PAGE_SIZE = 128

N_MIXERS = 3
N_A_LAYERS = (DEPTH + 2) // 3
N_B_LAYERS = (DEPTH + 1) // 3
N_C_LAYERS = DEPTH // 3
NORM_EPS = 1e-6
NEG_INF = -1e30

A_HEADS = 16
A_HEAD_DIM = D_MODEL // A_HEADS
MOBA_BLOCK = 256
MOBA_TOPK = 3
MOBA_QCHUNK = 16

B_HEADS = 4
B_DK = D_MODEL // 2 // B_HEADS
B_DV = D_MODEL // B_HEADS
B_GATE_RANK = 16
B_GATE_TAU = 16.0
B_CHUNK = 64

C_HEADS = 8
C_DK = D_MODEL // C_HEADS
C_DV = D_MODEL // C_HEADS
C_CONV = 4
C_CHUNK = 64

D_FF = 2816
FFN_CONV = 3

kernel_name = "moba_gla_gdn_convffn_hybrid_step"


def rmsnorm(x, g):
    x32 = x.astype(jnp.float32)
    y = x32 * lax.rsqrt(jnp.mean(x32 * x32, axis=-1, keepdims=True) + NORM_EPS)
    return (y * g.astype(jnp.float32)).astype(x.dtype)


def head_rmsnorm(o, g):
    return o * lax.rsqrt(jnp.mean(o * o, axis=-1, keepdims=True) + NORM_EPS) * g.astype(jnp.float32)


def l2norm(x):
    return x * lax.rsqrt(jnp.sum(x * x, axis=-1, keepdims=True) + NORM_EPS)


def split_heads(a, nh):
    b, t, _ = a.shape
    return a.reshape(b, t, nh, -1).transpose(0, 2, 1, 3)


def merge_heads(o, dtype):
    b, nh, t, d = o.shape
    return o.transpose(0, 2, 1, 3).reshape(b, t, nh * d).astype(dtype)


def causal_dwconv(u, prev, w):
    width = w.shape[0]
    t = u.shape[1]
    up = jnp.concatenate([prev.astype(u.dtype), u], axis=1)
    out = sum(up[:, j:j + t] * w[j] for j in range(width))
    return out, up[:, t:]


def conv_ffn(h, prev, w_in, conv_w, conv_b, w_out):
    u = h @ w_in
    u_c, new_prev = causal_dwconv(u, prev, conv_w)
    u_c = u_c + conv_b
    gate, val = jnp.split(u_c, 2, axis=-1)
    return (jax.nn.silu(gate) * val) @ w_out, new_prev


def gather_blocks(blocks, idx):
    b, nh = idx.shape[:2]
    bi = jnp.arange(b)[:, None, None, None]
    hi = jnp.arange(nh)[None, :, None, None]
    return blocks[bi, hi, idx]


def moba_attend(q, k_sel, v_sel, sel_valid, k_own, v_own, own_mask):
    scale = A_HEAD_DIM ** -0.5
    s_sel = jnp.einsum('bhqd,bhqkld->bhqkl', q, k_sel).astype(jnp.float32) * scale
    s_sel = jnp.where(sel_valid[..., None], s_sel, NEG_INF)
    s_own = jnp.einsum('bhqd,bhrd->bhqr', q, k_own).astype(jnp.float32) * scale
    s_own = jnp.where(own_mask, s_own, NEG_INF)
    lead = q.shape[:3]
    n_sel = s_sel.shape[3] * s_sel.shape[4]
    p = jax.nn.softmax(jnp.concatenate([s_sel.reshape(lead + (n_sel,)), s_own], axis=-1), axis=-1).astype(q.dtype)
    p_sel = p[..., :n_sel].reshape(s_sel.shape)
    p_own = p[..., n_sel:]
    return (jnp.einsum('bhqkl,bhqkld->bhqd', p_sel, v_sel)
            + jnp.einsum('bhqr,bhrd->bhqd', p_own, v_own))


def moba_qkv(h, w_in):
    q, k, v = jnp.split(h @ w_in, 3, axis=-1)
    return split_heads(q, A_HEADS), split_heads(k, A_HEADS), split_heads(v, A_HEADS)


def moba_prompt(h, w_in, w_out):
    bsz, s, _ = h.shape
    q, k, v = moba_qkv(h, w_in)
    nb = -(-s // MOBA_BLOCK)
    pad = nb * MOBA_BLOCK - s
    kb = jnp.pad(k, ((0, 0), (0, 0), (0, pad), (0, 0))).reshape(bsz, A_HEADS, nb, MOBA_BLOCK, A_HEAD_DIM)
    vb = jnp.pad(v, ((0, 0), (0, 0), (0, pad), (0, 0))).reshape(bsz, A_HEADS, nb, MOBA_BLOCK, A_HEAD_DIM)
    kmean = jnp.mean(kb.astype(jnp.float32), axis=3).astype(k.dtype)
    topk = min(MOBA_TOPK, nb)

    def one_chunk(c):
        start = c * MOBA_QCHUNK
        blk = start // MOBA_BLOCK
        qc = lax.dynamic_slice_in_dim(q, start, MOBA_QCHUNK, axis=2)
        gate = jnp.einsum('bhqd,bhnd->bhqn', qc, kmean).astype(jnp.float32)
        gate = jnp.where(jnp.arange(nb) < blk, gate, NEG_INF)
        _, idx = lax.top_k(gate, topk)
        valid = jnp.arange(topk) < blk
        k_sel = gather_blocks(kb, idx)
        v_sel = gather_blocks(vb, idx)
        k_own = lax.dynamic_index_in_dim(kb, blk, axis=2, keepdims=False)
        v_own = lax.dynamic_index_in_dim(vb, blk, axis=2, keepdims=False)
        q_pos = start + jnp.arange(MOBA_QCHUNK)
        k_pos = blk * MOBA_BLOCK + jnp.arange(MOBA_BLOCK)
        own_mask = k_pos[None, :] <= q_pos[:, None]
        return moba_attend(qc, k_sel, v_sel, valid, k_own, v_own, own_mask)

    o = lax.map(one_chunk, jnp.arange(s // MOBA_QCHUNK))
    o = o.transpose(1, 0, 3, 2, 4).reshape(bsz, s, D_MODEL)
    return o @ w_out, k.transpose(0, 2, 1, 3), v.transpose(0, 2, 1, 3)


def moba_sample(h, k_pool, v_pool, page_table, w_in, w_out):
    bsz, t, _ = h.shape
    q, k, v = moba_qkv(h, w_in)
    past = page_table.shape[1] * k_pool.shape[1]
    k_past = k_pool[page_table].reshape(bsz, past, A_HEADS, A_HEAD_DIM).transpose(0, 2, 1, 3)
    v_past = v_pool[page_table].reshape(bsz, past, A_HEADS, A_HEAD_DIM).transpose(0, 2, 1, 3)
    blk = past // MOBA_BLOCK
    r0 = blk * MOBA_BLOCK
    if blk > 0:
        kb = k_past[:, :, :r0].reshape(bsz, A_HEADS, blk, MOBA_BLOCK, A_HEAD_DIM)
        vb = v_past[:, :, :r0].reshape(bsz, A_HEADS, blk, MOBA_BLOCK, A_HEAD_DIM)
        kmean = jnp.mean(kb.astype(jnp.float32), axis=3).astype(k.dtype)
        gate = jnp.einsum('bhqd,bhnd->bhqn', q, kmean).astype(jnp.float32)
        topk = min(MOBA_TOPK, blk)
        _, idx = lax.top_k(gate, topk)
        k_sel = gather_blocks(kb, idx)
        v_sel = gather_blocks(vb, idx)
        valid = jnp.ones((topk,), dtype=bool)
    else:
        k_sel = jnp.zeros((bsz, A_HEADS, t, 0, MOBA_BLOCK, A_HEAD_DIM), k.dtype)
        v_sel = jnp.zeros((bsz, A_HEADS, t, 0, MOBA_BLOCK, A_HEAD_DIM), v.dtype)
        valid = jnp.ones((0,), dtype=bool)
    k_own = jnp.concatenate([k_past[:, :, r0:].astype(k.dtype), k], axis=2)
    v_own = jnp.concatenate([v_past[:, :, r0:].astype(v.dtype), v], axis=2)
    own_mask = jnp.concatenate([jnp.ones((t, past - r0), dtype=bool),
                                jnp.tril(jnp.ones((t, t), dtype=bool))], axis=1)
    o = moba_attend(q, k_sel, v_sel, valid, k_own, v_own, own_mask)
    return merge_heads(o, h.dtype) @ w_out, k.transpose(0, 2, 1, 3), v.transpose(0, 2, 1, 3)


def gla_chunk_scan(q, k, v, g, s0):
    bsz, nh, t, _ = q.shape
    dv = v.shape[-1]
    c = math.gcd(t, B_CHUNK)
    n = t // c

    def chunks(a):
        return a.reshape(bsz, nh, n, c, a.shape[-1]).transpose(2, 0, 1, 3, 4)

    causal = jnp.tril(jnp.ones((c, c), dtype=bool))

    def step(s, inp):
        qc, kc, vc, gc = inp
        b = jnp.cumsum(gc, axis=2)
        b_last = b[:, :, -1:, :]
        q_t = qc * jnp.exp(b)
        k_t = kc * jnp.exp(-b)
        k_end = kc * jnp.exp(b_last - b)
        att = jnp.where(causal, jnp.einsum('bhcd,bhed->bhce', q_t, k_t), 0.0)
        o = jnp.einsum('bhce,bhev->bhcv', att, vc) + jnp.einsum('bhcd,bhdv->bhcv', q_t, s)
        s = jnp.exp(b_last)[:, :, 0, :, None] * s + jnp.einsum('bhcd,bhcv->bhdv', k_end, vc)
        return s, o

    s, o = lax.scan(step, s0, (chunks(q), chunks(k), chunks(v), chunks(g)))
    return o.transpose(1, 2, 0, 3, 4).reshape(bsz, nh, t, dv), s


def gla_mixer(h, s0, w_in, w_gate, b_gate, norm_g, w_out):
    nk = B_HEADS * B_DK
    proj = h @ w_in
    q, k, v, r, glr = jnp.split(proj, [nk, 2 * nk, 2 * nk + D_MODEL, 2 * nk + 2 * D_MODEL], axis=-1)
    gk = jax.nn.log_sigmoid((glr @ w_gate + b_gate).astype(jnp.float32)) / B_GATE_TAU
    f32 = jnp.float32
    o, s = gla_chunk_scan(split_heads(q, B_HEADS).astype(f32) * (B_DK ** -0.5),
                          split_heads(k, B_HEADS).astype(f32),
                          split_heads(v, B_HEADS).astype(f32),
                          split_heads(gk, B_HEADS),
                          s0.astype(f32))
    o = merge_heads(head_rmsnorm(o, norm_g), h.dtype) * jax.nn.silu(r)
    return o @ w_out, s


def gdn_chunk_scan(q, k, v, beta, g, s0):
    bsz, nh, t, _ = q.shape
    dv = v.shape[-1]
    c = math.gcd(t, C_CHUNK)
    n = t // c

    def chunks4(a):
        return a.reshape(bsz, nh, n, c, a.shape[-1]).transpose(2, 0, 1, 3, 4)

    def chunks3(a):
        return a.reshape(bsz, nh, n, c).transpose(2, 0, 1, 3)

    incl = jnp.tril(jnp.ones((c, c), dtype=bool))
    strict = jnp.tril(jnp.ones((c, c), dtype=bool), -1)
    eye = jnp.eye(c, dtype=jnp.float32)

    def step(s, inp):
        qc, kc, vc, bc, gc = inp
        gam = jnp.cumsum(gc, axis=-1)
        diff = gam[..., :, None] - gam[..., None, :]
        dec = jnp.where(incl, jnp.exp(jnp.where(incl, diff, 0.0)), 0.0)
        kb = kc * bc[..., None]
        a_mat = jnp.where(strict, jnp.einsum('bhcd,bhed->bhce', kb, kc) * dec, 0.0) + eye
        w = lax.linalg.triangular_solve(a_mat, kb * jnp.exp(gam)[..., None], left_side=True, lower=True, unit_diagonal=True)
        u = lax.linalg.triangular_solve(a_mat, vc * bc[..., None], left_side=True, lower=True, unit_diagonal=True)
        v_new = u - jnp.einsum('bhcd,bhdv->bhcv', w, s)
        att = jnp.einsum('bhcd,bhed->bhce', qc, kc) * dec
        o = (jnp.einsum('bhcd,bhdv->bhcv', qc * jnp.exp(gam)[..., None], s)
             + jnp.einsum('bhce,bhev->bhcv', att, v_new))
        g_last = gam[..., -1:]
        s = (jnp.exp(g_last)[..., None] * s
             + jnp.einsum('bhcd,bhcv->bhdv', kc * jnp.exp(g_last - gam)[..., None], v_new))
        return s, o

    s, o = lax.scan(step, s0, (chunks4(q), chunks4(k), chunks4(v), chunks3(beta), chunks3(g)))
    return o.transpose(1, 2, 0, 3, 4).reshape(bsz, nh, t, dv), s


def gdn_mixer(h, s0, conv_prev, w_in, conv_w, a_log, dt_bias, norm_g, w_out):
    proj = h @ w_in
    qkv, z, b, a = jnp.split(proj, [3 * D_MODEL, 4 * D_MODEL, 4 * D_MODEL + C_HEADS], axis=-1)
    qkv, conv_new = causal_dwconv(qkv, conv_prev, conv_w)
    q, k, v = jnp.split(jax.nn.silu(qkv), 3, axis=-1)
    f32 = jnp.float32
    q = l2norm(split_heads(q, C_HEADS).astype(f32)) * (C_DK ** -0.5)
    k = l2norm(split_heads(k, C_HEADS).astype(f32))
    v = split_heads(v, C_HEADS).astype(f32)
    beta = jax.nn.sigmoid(b.astype(f32)).transpose(0, 2, 1)
    g = (-jnp.exp(a_log.astype(f32)) * jax.nn.softplus(a.astype(f32) + dt_bias.astype(f32))).transpose(0, 2, 1)
    o, s = gdn_chunk_scan(q, k, v, beta, g, s0.astype(f32))
    o = merge_heads(head_rmsnorm(o, norm_g), h.dtype) * jax.nn.silu(z)
    return o @ w_out, s, conv_new


def setup_inputs(seed: int = 0) -> dict:
    key = jax.random.key(seed)
    ks = jax.random.split(key, 40)
    f32 = jnp.float32
    n_pages = PAST_LEN // PAGE_SIZE
    n_pool = (DEC_BATCH * n_pages * 5) // 4

    def nrm(k, shape, scale):
        return jax.random.normal(k, shape, f32) * scale

    nk = B_HEADS * B_DK
    gla_in_width = 2 * nk + 2 * D_MODEL + B_GATE_RANK
    gdn_in_width = 4 * D_MODEL + 2 * C_HEADS
    dt = jnp.exp(jax.random.uniform(ks[24], (N_C_LAYERS, C_HEADS), f32, math.log(1e-3), math.log(1e-1)))
    return {
        'x_prompt': nrm(ks[0], (BATCH, SEQ, D_MODEL), 1.0),
        'x_sample': nrm(ks[1], (DEC_BATCH, DEC_SEQ, D_MODEL), 1.0),
        'cache_k': nrm(ks[2], (N_A_LAYERS, n_pool, PAGE_SIZE, A_HEADS, A_HEAD_DIM), 1.0),
        'cache_v': nrm(ks[3], (N_A_LAYERS, n_pool, PAGE_SIZE, A_HEADS, A_HEAD_DIM), 1.0),
        'page_table': jax.random.permutation(ks[4], n_pool)[:DEC_BATCH * n_pages].reshape(DEC_BATCH, n_pages).astype(jnp.int32),
        'state_gla': nrm(ks[5], (N_B_LAYERS, DEC_BATCH, B_HEADS, B_DK, B_DV), 0.5),
        'state_gdn': nrm(ks[6], (N_C_LAYERS, DEC_BATCH, C_HEADS, C_DK, C_DV), 0.1),
        'state_gdn_conv': nrm(ks[7], (N_C_LAYERS, DEC_BATCH, C_CONV - 1, 3 * D_MODEL), 1.0),
        'state_ffn_conv': nrm(ks[8], (DEPTH, DEC_BATCH, FFN_CONV - 1, 2 * D_FF), 1.0),
        'norm_mix': 1.0 + nrm(ks[9], (DEPTH, D_MODEL), 0.02),
        'norm_ffn': 1.0 + nrm(ks[10], (DEPTH, D_MODEL), 0.02),
        'norm_final': 1.0 + nrm(ks[11], (D_MODEL,), 0.02),
        'moba_w_in': nrm(ks[12], (N_A_LAYERS, D_MODEL, 3 * D_MODEL), D_MODEL ** -0.5),
        'moba_w_out': nrm(ks[13], (N_A_LAYERS, D_MODEL, D_MODEL), D_MODEL ** -0.5),
        'gla_w_in': nrm(ks[14], (N_B_LAYERS, D_MODEL, gla_in_width), D_MODEL ** -0.5),
        'gla_w_gate': nrm(ks[15], (N_B_LAYERS, B_GATE_RANK, nk), B_GATE_RANK ** -0.5),
        'gla_b_gate': nrm(ks[16], (N_B_LAYERS, nk), 0.1),
        'gla_norm': 1.0 + nrm(ks[17], (N_B_LAYERS, B_DV), 0.02),
        'gla_w_out': nrm(ks[18], (N_B_LAYERS, D_MODEL, D_MODEL), D_MODEL ** -0.5),
        'gdn_w_in': nrm(ks[19], (N_C_LAYERS, D_MODEL, gdn_in_width), D_MODEL ** -0.5),
        'gdn_conv_w': nrm(ks[20], (N_C_LAYERS, C_CONV, 3 * D_MODEL), C_CONV ** -0.5),
        'gdn_a_log': jnp.log(jax.random.uniform(ks[21], (N_C_LAYERS, C_HEADS), f32, 1.0, 16.0)),
        'gdn_dt_bias': dt + jnp.log(-jnp.expm1(-dt)),
        'gdn_norm': 1.0 + nrm(ks[22], (N_C_LAYERS, C_DV), 0.02),
        'gdn_w_out': nrm(ks[23], (N_C_LAYERS, D_MODEL, D_MODEL), D_MODEL ** -0.5),
        'ffn_w_in': nrm(ks[25], (DEPTH, D_MODEL, 2 * D_FF), D_MODEL ** -0.5),
        'ffn_conv_w': nrm(ks[26], (DEPTH, FFN_CONV, 2 * D_FF), FFN_CONV ** -0.5),
        'ffn_conv_b': nrm(ks[27], (DEPTH, 2 * D_FF), 0.01),
        'ffn_w_out': nrm(ks[28], (DEPTH, D_FF, D_MODEL), D_FF ** -0.5),
    }


def reference(x_prompt, x_sample, cache_k, cache_v, page_table, state_gla, state_gdn, state_gdn_conv,
              state_ffn_conv, norm_mix, norm_ffn, norm_final, moba_w_in, moba_w_out, gla_w_in, gla_w_gate,
              gla_b_gate, gla_norm, gla_w_out, gdn_w_in, gdn_conv_w, gdn_a_log, gdn_dt_bias, gdn_norm,
              gdn_w_out, ffn_w_in, ffn_conv_w, ffn_conv_b, ffn_w_out):
    xp, xs = x_prompt, x_sample
    bp = xp.shape[0]
    f32 = jnp.float32
    k_p, v_p, k_s, v_s = [], [], [], []
    gla_p, gla_s, gdn_p, gdn_s, gconv_p, gconv_s = [], [], [], [], [], []
    fconv_p, fconv_s = [], []
    for i in range(DEPTH):
        j = i // N_MIXERS
        hp = rmsnorm(xp, norm_mix[i])
        hs = rmsnorm(xs, norm_mix[i])
        if i % N_MIXERS == 0:
            mp, kr, vr = moba_prompt(hp, moba_w_in[j], moba_w_out[j])
            ms, kr2, vr2 = moba_sample(hs, cache_k[j], cache_v[j], page_table, moba_w_in[j], moba_w_out[j])
            k_p.append(kr); v_p.append(vr); k_s.append(kr2); v_s.append(vr2)
        elif i % N_MIXERS == 1:
            s0 = jnp.zeros((bp, B_HEADS, B_DK, B_DV), f32)
            mp, sp = gla_mixer(hp, s0, gla_w_in[j], gla_w_gate[j], gla_b_gate[j], gla_norm[j], gla_w_out[j])
            ms, ss = gla_mixer(hs, state_gla[j], gla_w_in[j], gla_w_gate[j], gla_b_gate[j], gla_norm[j], gla_w_out[j])
            gla_p.append(sp); gla_s.append(ss)
        else:
            s0 = jnp.zeros((bp, C_HEADS, C_DK, C_DV), f32)
            c0 = jnp.zeros((bp, C_CONV - 1, 3 * D_MODEL), xp.dtype)
            mp, sp, cp = gdn_mixer(hp, s0, c0, gdn_w_in[j], gdn_conv_w[j], gdn_a_log[j], gdn_dt_bias[j], gdn_norm[j], gdn_w_out[j])
            ms, ss, cs = gdn_mixer(hs, state_gdn[j], state_gdn_conv[j], gdn_w_in[j], gdn_conv_w[j], gdn_a_log[j], gdn_dt_bias[j], gdn_norm[j], gdn_w_out[j])
            gdn_p.append(sp); gdn_s.append(ss); gconv_p.append(cp); gconv_s.append(cs)
        xp = xp + mp
        xs = xs + ms
        hp = rmsnorm(xp, norm_ffn[i])
        hs = rmsnorm(xs, norm_ffn[i])
        fp, cfp = conv_ffn(hp, jnp.zeros((bp, FFN_CONV - 1, 2 * D_FF), xp.dtype), ffn_w_in[i], ffn_conv_w[i], ffn_conv_b[i], ffn_w_out[i])
        fs, cfs = conv_ffn(hs, state_ffn_conv[i], ffn_w_in[i], ffn_conv_w[i], ffn_conv_b[i], ffn_w_out[i])
        fconv_p.append(cfp); fconv_s.append(cfs)
        xp = xp + fp
        xs = xs + fs
    y_prompt = rmsnorm(xp, norm_final)
    y_sample = rmsnorm(xs, norm_final)
    return (y_prompt, y_sample,
            jnp.stack(k_p), jnp.stack(v_p), jnp.stack(k_s), jnp.stack(v_s),
            jnp.stack(gla_p), jnp.stack(gla_s), jnp.stack(gdn_p), jnp.stack(gdn_s),
            jnp.stack(gconv_p), jnp.stack(gconv_s), jnp.stack(fconv_p), jnp.stack(fconv_s))
```

```python
import functools
import math

import jax
import jax.numpy as jnp
from jax import lax
from jax.experimental import pallas as pl
from jax.experimental.pallas import tpu as pltpu

F32 = jnp.float32
BF16 = jnp.bfloat16
HIGHEST = lax.Precision.HIGHEST

D_MODEL = 1024
DEPTH = 4
N_MIXERS = 3
NORM_EPS = 1e-6
NEG_INF = -1e30
PAGE_SIZE = 128

A_HEADS = 16
A_HEAD_DIM = D_MODEL // A_HEADS
MOBA_BLOCK = 256
MOBA_TOPK = 3

B_HEADS = 4
B_DK = D_MODEL // 2 // B_HEADS
B_DV = D_MODEL // B_HEADS
B_GATE_RANK = 16
B_GATE_TAU = 16.0
B_CHUNK = 64

C_HEADS = 8
C_DK = D_MODEL // C_HEADS
C_DV = D_MODEL // C_HEADS
C_CONV = 4
C_CHUNK = 64

D_FF = 2816
FFN_CONV = 3

LANES = 128
SUBLANES = 8
ROW_TILE = 256
FFN_COL_CHUNK = 1408
SAMPLE_PAGES_PER_STEP = 2
VMEM_LIMIT = 56 * 1024 * 1024

NT_DIMS = (((1,), (1,)), ((), ()))
TN_DIMS = (((0,), (0,)), ((), ()))


def _params(sem, vmem=VMEM_LIMIT):
    return pltpu.CompilerParams(dimension_semantics=sem, vmem_limit_bytes=vmem)


def _const_spec(shape):
    nd = len(shape)
    return pl.BlockSpec(shape, lambda *_: (0,) * nd, pipeline_mode=pl.Buffered(1))


def _rms(x, g):
    return x * lax.rsqrt(jnp.mean(x * x, axis=-1, keepdims=True) + NORM_EPS) * g


def _sigmoid(x):
    return 1.0 / (1.0 + jnp.exp(-x))


def _silu(x):
    return x * _sigmoid(x)


def _softplus(x):
    return jnp.maximum(x, 0.0) + jnp.log(1.0 + jnp.exp(-jnp.abs(x)))


def _bdot(a, b):
    return jnp.dot(a.astype(BF16), b.astype(BF16), preferred_element_type=F32)


def _bdot_nt(a, b):
    return lax.dot_general(a.astype(BF16), b.astype(BF16), NT_DIMS, preferred_element_type=F32)


def _bdot_tn(a, b):
    return lax.dot_general(a.astype(BF16), b.astype(BF16), TN_DIMS, preferred_element_type=F32)


def _cumsum_rows(x):
    n = x.shape[0]
    row = lax.broadcasted_iota(jnp.int32, x.shape, 0)
    s = 1
    while s < n:
        x = x + jnp.where(row >= s, pltpu.roll(x, s, 0), 0.0)
        s *= 2
    return x


def _shift_rows(u, tail, s):
    r = pltpu.roll(u, s, 0)
    row = lax.broadcasted_iota(jnp.int32, tail.shape, 0)
    head = jnp.where(row < s, pltpu.roll(tail, s, 0), r[0:SUBLANES])
    return jnp.concatenate([head, r[SUBLANES:]], axis=0)


def _norm_proj_kernel(x_ref, g_ref, w_ref, *out_refs, splits):
    hb = _rms(x_ref[...], g_ref[...]).astype(BF16)
    for o_ref, (c0, c1) in zip(out_refs, splits):
        o_ref[...] = jnp.dot(hb, w_ref[:, c0:c1], preferred_element_type=F32).astype(o_ref.dtype)


def _norm_proj(x2d, g, w, splits):
    n = x2d.shape[0]
    tm = min(ROW_TILE, n)
    out_shape = [jax.ShapeDtypeStruct((n, c1 - c0), F32) for c0, c1 in splits]
    out_specs = [pl.BlockSpec((tm, c1 - c0), lambda i: (i, 0)) for c0, c1 in splits]
    return pl.pallas_call(
        functools.partial(_norm_proj_kernel, splits=splits),
        grid=(n // tm,),
        in_specs=[pl.BlockSpec((tm, D_MODEL), lambda i: (i, 0)),
                  _const_spec((1, D_MODEL)),
                  _const_spec(w.shape)],
        out_specs=out_specs,
        out_shape=out_shape,
        compiler_params=_params(("parallel",)),
        name="norm_proj",
    )(x2d, g.reshape(1, D_MODEL), w)


def _final_norm_kernel(x_ref, g_ref, o_ref):
    o_ref[...] = _rms(x_ref[...], g_ref[...])


def _final_norm(x2d, g):
    n = x2d.shape[0]
    tm = min(2 * ROW_TILE, n)
    return pl.pallas_call(
        _final_norm_kernel,
        grid=(n // tm,),
        in_specs=[pl.BlockSpec((tm, D_MODEL), lambda i: (i, 0)), _const_spec((1, D_MODEL))],
        out_specs=pl.BlockSpec((tm, D_MODEL), lambda i: (i, 0)),
        out_shape=jax.ShapeDtypeStruct((n, D_MODEL), F32),
        compiler_params=_params(("parallel",)),
        name="final_norm",
    )(x2d, g.reshape(1, D_MODEL))


def _post_kernel(*refs, per_seq_prev, tiles_per_seq):
    if per_seq_prev:
        (x_ref, o_ref, wo_ref, g_ref, win_ref, cw_ref, cb_ref, wout_ref, p1_ref, p2_ref,
         y_ref, u_ref) = refs
    else:
        (x_ref, o_ref, wo_ref, g_ref, win_ref, cw_ref, cb_ref, wout_ref,
         y_ref, tail_ref, carry_scr) = refs

        @pl.when(pl.program_id(0) % tiles_per_seq == 0)
        def _():
            carry_scr[...] = jnp.zeros_like(carry_scr)

    tm = x_ref.shape[0]
    x = x_ref[...] + jnp.dot(o_ref[...].astype(BF16), wo_ref[...], preferred_element_type=F32)
    hb = _rms(x, g_ref[...]).astype(BF16)
    if per_seq_prev:
        t_in_seq = lax.broadcasted_iota(jnp.int32, (tm, 1), 0) % SUBLANES
    acc = x
    for c in range(D_FF // FFN_COL_CHUNK):
        conv = []
        for off in (c * FFN_COL_CHUNK, D_FF + c * FFN_COL_CHUNK):
            cols = slice(off, off + FFN_COL_CHUNK)
            u = jnp.dot(hb, win_ref[:, cols], preferred_element_type=F32)
            if per_seq_prev:
                u1 = jnp.where(t_in_seq < 1, p1_ref[:, cols], pltpu.roll(u, 1, 0))
                u2 = jnp.where(t_in_seq < 2, p2_ref[:, cols], pltpu.roll(u, 2, 0))
                u_ref[:, cols] = u
            else:
                tail = carry_scr[:, cols]
                u1 = _shift_rows(u, tail, 1)
                u2 = _shift_rows(u, tail, 2)
                carry_scr[:, cols] = u[tm - SUBLANES:tm]
                tail_ref[:, cols] = u[tm - SUBLANES:tm]
            conv.append(u2 * cw_ref[0:1, cols] + u1 * cw_ref[1:2, cols] + u * cw_ref[2:3, cols]
                        + cb_ref[:, cols])
        act = (_silu(conv[0]) * conv[1]).astype(BF16)
        acc = acc + jnp.dot(act, wout_ref[c * FFN_COL_CHUNK:(c + 1) * FFN_COL_CHUNK, :],
                            preferred_element_type=F32)
    y_ref[...] = acc


def _post(x2d, o2d, w_o, g, w_in, conv_w, conv_b, w_out, seq_len, prev=None):
    n = x2d.shape[0]
    tm = min(ROW_TILE, n)
    nseq = n // seq_len
    row_spec = pl.BlockSpec((tm, D_MODEL), lambda i: (i, 0))
    in_specs = [row_spec, row_spec,
                _const_spec((D_MODEL, D_MODEL)), _const_spec((1, D_MODEL)),
                _const_spec((D_MODEL, 2 * D_FF)), _const_spec((FFN_CONV, 2 * D_FF)),
                _const_spec((1, 2 * D_FF)), _const_spec((D_FF, D_MODEL))]
    args = [x2d, o2d, w_o, g.reshape(1, D_MODEL), w_in, conv_w, conv_b.reshape(1, 2 * D_FF), w_out]
    y_shape = jax.ShapeDtypeStruct((n, D_MODEL), F32)
    if prev is None:
        assert seq_len % tm == 0
        tps = seq_len // tm
        y, tail = pl.pallas_call(
            functools.partial(_post_kernel, per_seq_prev=False, tiles_per_seq=tps),
            grid=(n // tm,),
            in_specs=in_specs,
            out_specs=[row_spec, pl.BlockSpec((None, SUBLANES, 2 * D_FF), lambda i: (i // tps, 0, 0))],
            out_shape=[y_shape, jax.ShapeDtypeStruct((nseq, SUBLANES, 2 * D_FF), F32)],
            scratch_shapes=[pltpu.VMEM((SUBLANES, 2 * D_FF), F32)],
            compiler_params=_params(("arbitrary",)),
            name="post_ffn_prompt",
        )(*args)
        return y, tail[:, SUBLANES - (FFN_CONV - 1):]
    assert seq_len == SUBLANES and n == tm
    pad = ((0, 0), (0, SUBLANES - 1), (0, 0))
    p1 = jnp.pad(prev[:, 1:2], pad).reshape(n, 2 * D_FF)
    p2 = jnp.pad(prev, ((0, 0), (0, SUBLANES - 2), (0, 0))).reshape(n, 2 * D_FF)
    wide_spec = pl.BlockSpec((tm, 2 * D_FF), lambda i: (i, 0))
    y, u = pl.pallas_call(
        functools.partial(_post_kernel, per_seq_prev=True, tiles_per_seq=1),
        grid=(1,),
        in_specs=in_specs + [wide_spec, wide_spec],
        out_specs=[row_spec, wide_spec],
        out_shape=[y_shape, jax.ShapeDtypeStruct((n, 2 * D_FF), F32)],
        compiler_params=_params(("arbitrary",)),
        name="post_ffn_sample",
    )(*args, p1, p2)
    return y, u.reshape(nseq, seq_len, 2 * D_FF)[:, seq_len - (FFN_CONV - 1):]


def _topk_select(gate, n_valid, n_cols):
    col = lax.broadcasted_iota(jnp.int32, gate.shape, 1)
    valid = col < n_valid
    sel = []
    for n in range(n_cols):
        gn = gate[:, n:n + 1]
        beats = jnp.where(gate > gn, 1.0, jnp.where((gate == gn) & (col < n), 1.0, 0.0))
        rank = jnp.sum(jnp.where(valid, beats, 0.0), axis=1, keepdims=True)
        sel.append(rank < MOBA_TOPK)
    return sel


def _moba_prompt_kernel(q_ref, k_ref, v_ref, o_ref, kb_scr, vb_scr, km_scr, m_scr, l_scr, acc_scr,
                        *, nb):
    i = pl.program_id(2)
    blk = MOBA_BLOCK

    @pl.when(i == 0)
    def _():
        kb_scr[...] = k_ref[...].astype(BF16)
        vb_scr[...] = v_ref[...].astype(BF16)
        km_scr[...] = jnp.zeros_like(km_scr)
        for n in range(nb):
            km_scr[n:n + 1, :] = jnp.mean(k_ref[n * blk:(n + 1) * blk, :], axis=0, keepdims=True)

    q = q_ref[...]
    lane = lax.broadcasted_iota(jnp.int32, (1, LANES), 1)
    row = lax.broadcasted_iota(jnp.int32, (blk, blk), 0)
    colk = lax.broadcasted_iota(jnp.int32, (blk, blk), 1)
    own_start = pl.multiple_of(i * blk, blk)
    k_own = kb_scr[pl.ds(own_start, blk), :]
    v_own = vb_scr[pl.ds(own_start, blk), :]
    km = km_scr[...]
    for hh in range(2):
        qh = jnp.where(lane // A_HEAD_DIM == hh, q, 0.0)
        gate = lax.dot_general(qh, km, NT_DIMS, precision=HIGHEST, preferred_element_type=F32)
        sel = _topk_select(gate, i, nb - 1)
        qs = (qh * (A_HEAD_DIM ** -0.5)).astype(BF16)
        s = lax.dot_general(qs, k_own, NT_DIMS, preferred_element_type=F32)
        s = jnp.where(colk <= row, s, NEG_INF)
        m = jnp.max(s, axis=1, keepdims=True)
        p = jnp.exp(s - m)
        m_scr[hh] = m
        l_scr[hh] = jnp.sum(p, axis=1, keepdims=True)
        acc_scr[hh] = jnp.dot(p.astype(BF16), v_own, preferred_element_type=F32)
        for n in range(nb - 1):
            @pl.when(n < i)
            def _(n=n, hh=hh, qs=qs, sel=sel):
                kn = kb_scr[n * blk:(n + 1) * blk, :]
                vn = vb_scr[n * blk:(n + 1) * blk, :]
                s = lax.dot_general(qs, kn, NT_DIMS, preferred_element_type=F32)
                s = jnp.where(sel[n], s, NEG_INF)
                m_old = m_scr[hh]
                m_new = jnp.maximum(m_old, jnp.max(s, axis=1, keepdims=True))
                alpha = jnp.exp(m_old - m_new)
                p = jnp.exp(s - m_new)
                m_scr[hh] = m_new
                l_scr[hh] = alpha * l_scr[hh] + jnp.sum(p, axis=1, keepdims=True)
                acc_scr[hh] = alpha * acc_scr[hh] + jnp.dot(p.astype(BF16), vn,
                                                            preferred_element_type=F32)
    o0 = acc_scr[0] / l_scr[0]
    o1 = acc_scr[1] / l_scr[1]
    o_ref[...] = jnp.where(lane < A_HEAD_DIM, o0, o1).astype(o_ref.dtype)


def _moba_prompt_attn(q, k, v):
    bsz, s, _ = q.shape
    assert s % MOBA_BLOCK == 0
    nb = s // MOBA_BLOCK
    assert 1 <= nb <= LANES
    hp = D_MODEL // LANES
    q_spec = pl.BlockSpec((None, MOBA_BLOCK, LANES), lambda b, h, i: (b, i, h))
    kv_spec = pl.BlockSpec((None, s, LANES), lambda b, h, i: (b, 0, h))
    return pl.pallas_call(
        functools.partial(_moba_prompt_kernel, nb=nb),
        grid=(bsz, hp, nb),
        in_specs=[q_spec, kv_spec, kv_spec],
        out_specs=q_spec,
        out_shape=jax.ShapeDtypeStruct((bsz, s, D_MODEL), BF16),
        scratch_shapes=[pltpu.VMEM((s, LANES), BF16), pltpu.VMEM((s, LANES), BF16),
                        pltpu.VMEM((LANES, LANES), F32),
                        pltpu.VMEM((2, MOBA_BLOCK, 1), F32), pltpu.VMEM((2, MOBA_BLOCK, 1), F32),
                        pltpu.VMEM((2, MOBA_BLOCK, LANES), F32)],
        compiler_params=_params(("parallel", "parallel", "arbitrary")),
        name="moba_prompt_attn",
    )(q, k, v)


def _moba_sample_kernel(pt_ref, q_ref, kn_ref, vn_ref, *refs, n_pages, t_new):
    g_pages = SAMPLE_PAGES_PER_STEP
    k_refs = refs[:g_pages]
    v_refs = refs[g_pages:2 * g_pages]
    o_ref = refs[2 * g_pages]
    qbf_scr, new_scr, gate_scr, sc_scr, l_scr, acc_scr = refs[2 * g_pages + 1:]
    del pt_ref
    j = pl.program_id(1)
    n_steps = n_pages // g_pages
    n_blocks = n_pages * PAGE_SIZE // MOBA_BLOCK
    pages_per_block = MOBA_BLOCK // PAGE_SIZE
    rows = A_HEADS * t_new
    head_of_row = lax.broadcasted_iota(jnp.int32, (rows, D_MODEL), 0) // t_new
    head_of_lane = lax.broadcasted_iota(jnp.int32, (rows, D_MODEL), 1) // A_HEAD_DIM

    @pl.when(j == 0)
    def _():
        qbd = jnp.where(head_of_row == head_of_lane,
                        jnp.concatenate([q_ref[...]] * A_HEADS, axis=0), 0.0)
        qbf_scr[...] = (qbd * (A_HEAD_DIM ** -0.5)).astype(BF16)
        gate_scr[...] = jnp.zeros_like(gate_scr)

    @pl.when(j < n_steps)
    def _():
        gsum = jnp.zeros((rows, 1), F32)
        for g in range(g_pages):
            s = jnp.dot(qbf_scr[...], k_refs[g][...].astype(BF16), preferred_element_type=F32)
            sc_scr[j * g_pages + g] = s
            gsum = gsum + jnp.sum(s, axis=1, keepdims=True)
        col = lax.broadcasted_iota(jnp.int32, gate_scr.shape, 1)
        gate_scr[...] = jnp.where(col == j, gsum * (1.0 / MOBA_BLOCK), gate_scr[...])

    @pl.when(j == n_steps - 1)
    def _():
        sel = _topk_select(gate_scr[...], n_blocks, n_blocks)
        new_scr[...] = jnp.zeros_like(new_scr)
        new_scr[0, 0:t_new, :] = kn_ref[...].astype(BF16)
        new_scr[1, 0:t_new, :] = vn_ref[...].astype(BF16)
        s_own = lax.dot_general(qbf_scr[...], new_scr[0], NT_DIMS, preferred_element_type=F32)
        r_tok = lax.broadcasted_iota(jnp.int32, s_own.shape, 0) % t_new
        c_tok = lax.broadcasted_iota(jnp.int32, s_own.shape, 1)
        s_own = jnp.where(c_tok <= r_tok, s_own, NEG_INF)
        m = jnp.max(s_own, axis=1, keepdims=True)
        for p in range(n_pages):
            sp = jnp.where(sel[p // pages_per_block], sc_scr[p], NEG_INF)
            m = jnp.maximum(m, jnp.max(sp, axis=1, keepdims=True))
        p_own = jnp.exp(s_own - m)
        l = jnp.sum(p_own, axis=1, keepdims=True)
        for p in range(n_pages):
            e = jnp.exp(jnp.where(sel[p // pages_per_block], sc_scr[p], NEG_INF) - m)
            l = l + jnp.sum(e, axis=1, keepdims=True)
            sc_scr[p] = e
        l_scr[...] = l
        acc_scr[...] = jnp.dot(p_own.astype(BF16), new_scr[1], preferred_element_type=F32)

    @pl.when(j >= n_steps)
    def _():
        acc = acc_scr[...]
        for g in range(g_pages):
            pp = sc_scr[(j - n_steps) * g_pages + g].astype(BF16)
            acc = acc + lax.dot_general(pp, v_refs[g][...].astype(BF16), NT_DIMS,
                                        preferred_element_type=F32)
        acc_scr[...] = acc

    @pl.when(j == 2 * n_steps - 1)
    def _():
        acc = jnp.where(head_of_row == head_of_lane, acc_scr[...] / l_scr[...], 0.0)
        o = acc[0:t_new]
        for h in range(1, A_HEADS):
            o = o + acc[h * t_new:(h + 1) * t_new]
        o_ref[...] = o


def _moba_sample_attn(q, k_new, v_new, cache_kt, cache_vt, page_table, layer):
    bsz, t_new, _ = q.shape
    n_pages = page_table.shape[1]
    g_pages = SAMPLE_PAGES_PER_STEP
    assert (n_pages * PAGE_SIZE) % MOBA_BLOCK == 0 and MOBA_BLOCK % PAGE_SIZE == 0
    assert g_pages * PAGE_SIZE == MOBA_BLOCK and n_pages % g_pages == 0
    assert t_new == SUBLANES and A_HEADS * t_new == LANES and PAGE_SIZE == LANES
    n_steps = n_pages // g_pages
    n_blocks = n_pages * PAGE_SIZE // MOBA_BLOCK
    assert MOBA_TOPK <= n_blocks <= LANES
    tok_spec = pl.BlockSpec((None, t_new, D_MODEL), lambda b, j, pt: (b, 0, 0))

    def k_spec(g):
        return pl.BlockSpec(
            (None, None, D_MODEL, PAGE_SIZE),
            lambda b, j, pt: (layer, pt[b, g_pages * jnp.minimum(j, n_steps - 1) + g], 0, 0))

    def v_spec(g):
        return pl.BlockSpec(
            (None, None, D_MODEL, PAGE_SIZE),
            lambda b, j, pt: (layer, pt[b, g_pages * jnp.maximum(j - n_steps, 0) + g], 0, 0))

    rows = A_HEADS * t_new
    grid_spec = pltpu.PrefetchScalarGridSpec(
        num_scalar_prefetch=1,
        grid=(bsz, 2 * n_steps),
        in_specs=[tok_spec, tok_spec, tok_spec] + [k_spec(g) for g in range(g_pages)]
        + [v_spec(g) for g in range(g_pages)],
        out_specs=tok_spec,
        scratch_shapes=[pltpu.VMEM((rows, D_MODEL), BF16),
                        pltpu.VMEM((2, LANES, D_MODEL), BF16), pltpu.VMEM((rows, LANES), F32),
                        pltpu.VMEM((n_pages, rows, PAGE_SIZE), F32), pltpu.VMEM((rows, 1), F32),
                        pltpu.VMEM((rows, D_MODEL), F32)],
    )
    return pl.pallas_call(
        functools.partial(_moba_sample_kernel, n_pages=n_pages, t_new=t_new),
        grid_spec=grid_spec,
        out_shape=jax.ShapeDtypeStruct((bsz, t_new, D_MODEL), F32),
        compiler_params=_params(("parallel", "arbitrary")),
        name="moba_sample_attn",
    )(page_table, q, k_new, v_new, *([cache_kt] * g_pages), *([cache_vt] * g_pages))


def _gla_kernel(q_ref, k_ref, v_ref, r_ref, glr_ref, wg_ref, bg_ref, ng_ref, s0_ref,
                o_ref, sout_ref, s_scr, *, valid_rows):
    ci = pl.program_id(1)
    c = q_ref.shape[0]

    @pl.when(ci == 0)
    def _():
        s_scr[...] = s0_ref[...]

    x = jnp.dot(glr_ref[...].astype(BF16), wg_ref[...], preferred_element_type=F32) + bg_ref[...]
    gk = -_softplus(-x) * (1.0 / B_GATE_TAU)
    if valid_rows < c:
        gk = jnp.where(lax.broadcasted_iota(jnp.int32, gk.shape, 0) < valid_rows, gk, 0.0)
    b = _cumsum_rows(gk)
    b_last = b[c - 1:c, :]
    k = k_ref[...]
    q_t = (q_ref[...] * (B_DK ** -0.5) * jnp.exp(b)).astype(BF16)
    k_t = (k * jnp.exp(-b)).astype(BF16)
    k_end = (k * jnp.exp(b_last - b)).astype(BF16)
    e_last = jnp.exp(b_last)
    causal = (lax.broadcasted_iota(jnp.int32, (c, c), 1) <= lax.broadcasted_iota(jnp.int32, (c, c), 0))
    for h in range(B_HEADS):
        kc = slice(h * B_DK, (h + 1) * B_DK)
        vc = slice(h * B_DV, (h + 1) * B_DV)
        vh = v_ref[:, vc].astype(BF16)
        st = s_scr[h]
        att = jnp.where(causal, lax.dot_general(q_t[:, kc], k_t[:, kc], NT_DIMS,
                                                preferred_element_type=F32), 0.0)
        o = (jnp.dot(att.astype(BF16), vh, preferred_element_type=F32)
             + lax.dot_general(q_t[:, kc], st.astype(BF16), NT_DIMS, preferred_element_type=F32))
        s_scr[h] = st * e_last[:, kc] + lax.dot_general(vh, k_end[:, kc], TN_DIMS,
                                                        preferred_element_type=F32)
        o = o * lax.rsqrt(jnp.mean(o * o, axis=-1, keepdims=True) + NORM_EPS) * ng_ref[...]
        o_ref[:, vc] = (o * _silu(r_ref[:, vc])).astype(o_ref.dtype)

    @pl.when(ci == pl.num_programs(1) - 1)
    def _():
        sout_ref[...] = s_scr[...]


def _gla_scan(q, k, v, r, glr, w_gate, b_gate, norm_g, s0_t, valid_rows):
    bsz, t, _ = q.shape
    c = B_CHUNK
    assert t % c == 0
    nk = B_HEADS * B_DK

    def tok(width):
        return pl.BlockSpec((None, c, width), lambda b, ci: (b, ci, 0))

    st_spec = pl.BlockSpec((None, B_HEADS, B_DV, B_DK), lambda b, ci: (b, 0, 0, 0))
    return pl.pallas_call(
        functools.partial(_gla_kernel, valid_rows=valid_rows),
        grid=(bsz, t // c),
        in_specs=[tok(nk), tok(nk), tok(D_MODEL), tok(D_MODEL), tok(LANES),
                  _const_spec((LANES, nk)), _const_spec((1, nk)), _const_spec((1, B_DV)), st_spec],
        out_specs=[tok(D_MODEL), st_spec],
        out_shape=[jax.ShapeDtypeStruct((bsz, t, D_MODEL), BF16),
                   jax.ShapeDtypeStruct((bsz, B_HEADS, B_DV, B_DK), F32)],
        scratch_shapes=[pltpu.VMEM((B_HEADS, B_DV, B_DK), F32)],
        compiler_params=_params(("parallel", "arbitrary")),
        name="gla_scan",
    )(q, k, v, r, glr, w_gate, b_gate.reshape(1, nk), norm_g.reshape(1, B_DV), s0_t)


def _unit_lower_inverse(a_strict):
    c = a_strict.shape[0]
    eye = (lax.broadcasted_iota(jnp.int32, (c, c), 0)
           == lax.broadcasted_iota(jnp.int32, (c, c), 1)).astype(F32)
    x = -a_strict
    inv = eye + x
    p = 2
    while p < c:
        x = jnp.dot(x, x, precision=HIGHEST, preferred_element_type=F32)
        inv = inv + jnp.dot(inv, x, precision=HIGHEST, preferred_element_type=F32)
        p *= 2
    return inv


def _gdn_kernel(qkv_ref, z_ref, ba_ref, cw_ref, alog_ref, dtb_ref, ng_ref, s0_ref,
                o_ref, sout_ref, s_scr, carry_scr, *, valid_lo, valid_hi):
    ci = pl.program_id(1)
    c = qkv_ref.shape[0]

    @pl.when(ci == 0)
    def _():
        s_scr[...] = s0_ref[...]
        carry_scr[...] = jnp.zeros_like(carry_scr)

    u = qkv_ref[...]
    tail = carry_scr[...]
    conv = u * cw_ref[C_CONV - 1:C_CONV, :]
    for s in range(1, C_CONV):
        conv = conv + _shift_rows(u, tail, s) * cw_ref[C_CONV - 1 - s:C_CONV - s, :]
    carry_scr[...] = u[c - SUBLANES:c]
    xc = _silu(conv)

    ba = ba_ref[...]
    beta_all = _sigmoid(ba)
    g_all = -jnp.exp(alog_ref[...]) * _softplus(ba + dtb_ref[...])
    if valid_lo > 0 or valid_hi < c:
        rowi = lax.broadcasted_iota(jnp.int32, ba.shape, 0)
        ok = (rowi >= valid_lo) & (rowi < valid_hi)
        beta_all = jnp.where(ok, beta_all, 0.0)
        g_all = jnp.where(ok, g_all, 0.0)
    gam_all = _cumsum_rows(g_all)
    gam_t = jnp.concatenate([gam_all, jnp.zeros((LANES - c, LANES), F32)], axis=0).T
    ri = lax.broadcasted_iota(jnp.int32, (c, c), 0)
    cj = lax.broadcasted_iota(jnp.int32, (c, c), 1)
    incl = cj <= ri
    strict = cj < ri
    for h in range(C_HEADS):
        hc = slice(h * C_DK, (h + 1) * C_DK)
        qh = xc[:, hc]
        kh = xc[:, D_MODEL + h * C_DK:D_MODEL + (h + 1) * C_DK]
        vh = xc[:, 2 * D_MODEL + h * C_DV:2 * D_MODEL + (h + 1) * C_DV]
        qh = qh * lax.rsqrt(jnp.sum(qh * qh, axis=-1, keepdims=True) + NORM_EPS) * (C_DK ** -0.5)
        kh = kh * lax.rsqrt(jnp.sum(kh * kh, axis=-1, keepdims=True) + NORM_EPS)
        beta = beta_all[:, h:h + 1]
        gcol = gam_all[:, C_HEADS + h:C_HEADS + h + 1]
        grow = gam_t[C_HEADS + h:C_HEADS + h + 1, 0:c]
        dec = jnp.where(incl, jnp.exp(jnp.where(incl, gcol - grow, 0.0)), 0.0)
        kb = kh * beta
        a_mat = jnp.where(strict, lax.dot_general(kb, kh, NT_DIMS, precision=HIGHEST,
                                                  preferred_element_type=F32) * dec, 0.0)
        inv = _unit_lower_inverse(a_mat)
        rhs = jnp.concatenate([kb * jnp.exp(gcol), vh * beta], axis=1)
        wu = jnp.dot(inv, rhs, precision=HIGHEST, preferred_element_type=F32)
        st = s_scr[h]
        stb = st.astype(BF16)
        v_new = wu[:, C_DK:] - lax.dot_general(wu[:, :C_DK].astype(BF16), stb, NT_DIMS,
                                               preferred_element_type=F32)
        att = _bdot_nt(qh, kh) * dec
        o = (lax.dot_general((qh * jnp.exp(gcol)).astype(BF16), stb, NT_DIMS,
                             preferred_element_type=F32) + _bdot(att, v_new))
        g_last = gcol[c - 1:c, :]
        s_scr[h] = jnp.exp(g_last) * st + _bdot_tn(v_new, kh * jnp.exp(g_last - gcol))
        o = o * lax.rsqrt(jnp.mean(o * o, axis=-1, keepdims=True) + NORM_EPS) * ng_ref[...]
        o_ref[:, hc] = (o * _silu(z_ref[:, hc])).astype(o_ref.dtype)

    @pl.when(ci == pl.num_programs(1) - 1)
    def _():
        sout_ref[...] = s_scr[...]


def _gdn_scan(qkv, z, ba, conv_w, a_log, dt_bias, norm_g, s0_t, valid_lo, valid_hi):
    bsz, t, _ = qkv.shape
    c = C_CHUNK
    assert t % c == 0

    def tok(width):
        return pl.BlockSpec((None, c, width), lambda b, ci: (b, ci, 0))

    def under_a(p):
        return jnp.zeros((1, LANES), F32).at[0, C_HEADS:2 * C_HEADS].set(p)

    st_spec = pl.BlockSpec((None, C_HEADS, C_DV, C_DK), lambda b, ci: (b, 0, 0, 0))
    return pl.pallas_call(
        functools.partial(_gdn_kernel, valid_lo=valid_lo, valid_hi=valid_hi),
        grid=(bsz, t // c),
        in_specs=[tok(3 * D_MODEL), tok(D_MODEL), tok(LANES),
                  _const_spec((C_CONV, 3 * D_MODEL)), _const_spec((1, LANES)),
                  _const_spec((1, LANES)), _const_spec((1, C_DV)), st_spec],
        out_specs=[tok(D_MODEL), st_spec],
        out_shape=[jax.ShapeDtypeStruct((bsz, t, D_MODEL), BF16),
                   jax.ShapeDtypeStruct((bsz, C_HEADS, C_DV, C_DK), F32)],
        scratch_shapes=[pltpu.VMEM((C_HEADS, C_DV, C_DK), F32),
                        pltpu.VMEM((SUBLANES, 3 * D_MODEL), F32)],
        compiler_params=_params(("parallel", "arbitrary")),
        name="gdn_scan",
    )(qkv, z, ba, conv_w, under_a(a_log), under_a(dt_bias), norm_g.reshape(1, C_DV), s0_t)


def _pad_cols(w, width):
    return jnp.pad(w, ((0, 0), (0, width - w.shape[1])))


def _pad_tokens(a, before, total):
    return jnp.pad(a, ((0, 0), (before, total - before - a.shape[1]), (0, 0)))


def _moba_mixer(x, g, w_in, cache=None):
    bsz, t, _ = x.shape
    splits = tuple((i * D_MODEL, (i + 1) * D_MODEL) for i in range(3))
    q, k, v = (a.reshape(bsz, t, D_MODEL)
               for a in _norm_proj(x.reshape(bsz * t, D_MODEL), g, w_in.astype(BF16), splits))
    if cache is None:
        o = _moba_prompt_attn(q, k, v)
    else:
        o = _moba_sample_attn(q, k, v, *cache)
    rows = (bsz, t, A_HEADS, A_HEAD_DIM)
    return o.reshape(bsz * t, D_MODEL), k.reshape(rows), v.reshape(rows)


def _gla_mixer(x, g, w_in, w_gate, b_gate, norm_g, s0):
    bsz, t, _ = x.shape
    nk = B_HEADS * B_DK
    edges = (0, nk, 2 * nk, 2 * nk + D_MODEL, 2 * nk + 2 * D_MODEL, 2 * nk + 2 * D_MODEL + LANES)
    splits = tuple(zip(edges[:-1], edges[1:]))
    parts = _norm_proj(x.reshape(bsz * t, D_MODEL), g, _pad_cols(w_in, edges[-1]).astype(BF16), splits)
    t_pad = -(-t // B_CHUNK) * B_CHUNK
    q, k, v, r, glr = (_pad_tokens(a.reshape(bsz, t, -1), 0, t_pad) for a in parts)
    wg = jnp.pad(w_gate, ((0, LANES - B_GATE_RANK), (0, 0))).astype(BF16)
    if s0 is None:
        s0_t = jnp.zeros((bsz, B_HEADS, B_DV, B_DK), F32)
    else:
        s0_t = jnp.swapaxes(s0, -1, -2)
    o, s_t = _gla_scan(q, k, v, r, glr, wg, b_gate, norm_g, s0_t, min(t, B_CHUNK))
    return o[:, :t].reshape(bsz * t, D_MODEL), jnp.swapaxes(s_t, -1, -2)


def _gdn_mixer(x, g, w_in, conv_w, a_log, dt_bias, norm_g, s0, conv_prev):
    bsz, t, _ = x.shape
    edges = (0, 3 * D_MODEL, 4 * D_MODEL, 4 * D_MODEL + LANES)
    splits = tuple(zip(edges[:-1], edges[1:]))
    parts = _norm_proj(x.reshape(bsz * t, D_MODEL), g, _pad_cols(w_in, edges[-1]).astype(BF16), splits)
    qkv, z, ba = (a.reshape(bsz, t, -1) for a in parts)
    hist = C_CONV - 1
    if conv_prev is None:
        lo = 0
        conv_new = qkv[:, t - hist:]
        s0_t = jnp.zeros((bsz, C_HEADS, C_DV, C_DK), F32)
        qkv_in = qkv
    else:
        lo = hist
        assert t >= hist
        conv_new = qkv[:, t - hist:]
        s0_t = jnp.swapaxes(s0, -1, -2)
        qkv_in = jnp.concatenate([conv_prev, qkv], axis=1)
    t_pad = -(-(lo + t) // C_CHUNK) * C_CHUNK
    if t_pad != lo + t or lo:
        assert t_pad == C_CHUNK
    qkv_in = _pad_tokens(qkv_in, 0, t_pad)
    z = _pad_tokens(z, lo, t_pad)
    ba = _pad_tokens(ba, lo, t_pad)
    o, s_t = _gdn_scan(qkv_in, z, ba, conv_w, a_log, dt_bias, norm_g, s0_t, lo, min(lo + t, C_CHUNK))
    return o[:, lo:lo + t].reshape(bsz * t, D_MODEL), jnp.swapaxes(s_t, -1, -2), conv_new


def _run_group(x, cache, state_gla, state_gdn, state_gdn_conv, state_ffn_conv, w):
    bsz, t, _ = x.shape
    x2d = x.reshape(bsz * t, D_MODEL)
    out = {name: [] for name in ("k", "v", "gla", "gdn", "gconv", "fconv")}
    for i in range(DEPTH):
        j = i // N_MIXERS
        xin = x2d.reshape(bsz, t, D_MODEL)
        if i % N_MIXERS == 0:
            layer_cache = None if cache is None else (cache[0], cache[1], cache[2], j)
            o, kr, vr = _moba_mixer(xin, w["norm_mix"][i], w["moba_w_in"][j], layer_cache)
            out["k"].append(kr)
            out["v"].append(vr)
            w_o = w["moba_w_out"][j]
        elif i % N_MIXERS == 1:
            s0 = None if state_gla is None else state_gla[j]
            o, s = _gla_mixer(xin, w["norm_mix"][i], w["gla_w_in"][j], w["gla_w_gate"][j],
                              w["gla_b_gate"][j], w["gla_norm"][j], s0)
            out["gla"].append(s)
            w_o = w["gla_w_out"][j]
        else:
            s0 = None if state_gdn is None else state_gdn[j]
            cp = None if state_gdn_conv is None else state_gdn_conv[j]
            o, s, cn = _gdn_mixer(xin, w["norm_mix"][i], w["gdn_w_in"][j], w["gdn_conv_w"][j],
                                  w["gdn_a_log"][j], w["gdn_dt_bias"][j], w["gdn_norm"][j], s0, cp)
            out["gdn"].append(s)
            out["gconv"].append(cn)
            w_o = w["gdn_w_out"][j]
        prev = None if state_ffn_conv is None else state_ffn_conv[i]
        x2d, fc = _post(x2d, o, w_o.astype(BF16), w["norm_ffn"][i], w["ffn_w_in"][i].astype(BF16),
                        w["ffn_conv_w"][i], w["ffn_conv_b"][i], w["ffn_w_out"][i].astype(BF16), t, prev)
        out["fconv"].append(fc)
    y = _final_norm(x2d, w["norm_final"]).reshape(bsz, t, D_MODEL)
    return y, {name: jnp.stack(v) for name, v in out.items()}


def kernel(x_prompt, x_sample, cache_k, cache_v, page_table, state_gla, state_gdn, state_gdn_conv,
           state_ffn_conv, norm_mix, norm_ffn, norm_final, moba_w_in, moba_w_out, gla_w_in, gla_w_gate,
           gla_b_gate, gla_norm, gla_w_out, gdn_w_in, gdn_conv_w, gdn_a_log, gdn_dt_bias, gdn_norm,
           gdn_w_out, ffn_w_in, ffn_conv_w, ffn_conv_b, ffn_w_out):
    w = dict(norm_mix=norm_mix, norm_ffn=norm_ffn, norm_final=norm_final, moba_w_in=moba_w_in,
             moba_w_out=moba_w_out, gla_w_in=gla_w_in, gla_w_gate=gla_w_gate, gla_b_gate=gla_b_gate,
             gla_norm=gla_norm, gla_w_out=gla_w_out, gdn_w_in=gdn_w_in, gdn_conv_w=gdn_conv_w,
             gdn_a_log=gdn_a_log, gdn_dt_bias=gdn_dt_bias, gdn_norm=gdn_norm, gdn_w_out=gdn_w_out,
             ffn_w_in=ffn_w_in, ffn_conv_w=ffn_conv_w, ffn_conv_b=ffn_conv_b, ffn_w_out=ffn_w_out)
    n_layers, n_pool = cache_k.shape[:2]
    pool_shape = (n_layers, n_pool, D_MODEL, PAGE_SIZE)
    cache = (cache_k.transpose(0, 1, 3, 4, 2).reshape(pool_shape),
             cache_v.transpose(0, 1, 3, 4, 2).reshape(pool_shape), page_table)
    yp, p = _run_group(x_prompt, None, None, None, None, None, w)
    ys, s = _run_group(x_sample, cache, state_gla, state_gdn, state_gdn_conv, state_ffn_conv, w)
    return (yp, ys, p["k"], p["v"], s["k"], s["v"], p["gla"], s["gla"], p["gdn"], s["gdn"],
            p["gconv"], s["gconv"], p["fconv"], s["fconv"])
```

```python
import functools
import math

import jax
import jax.numpy as jnp
from jax import lax
from jax.experimental import pallas as pl
from jax.experimental.pallas import tpu as pltpu

F32 = jnp.float32
BF16 = jnp.bfloat16
HIGHEST = lax.Precision.HIGHEST

D_MODEL = 1024
DEPTH = 4
N_MIXERS = 3
NORM_EPS = 1e-6
NEG_INF = -1e30
PAGE_SIZE = 128

A_HEADS = 16
A_HEAD_DIM = D_MODEL // A_HEADS
MOBA_BLOCK = 256
MOBA_TOPK = 3

B_HEADS = 4
B_DK = D_MODEL // 2 // B_HEADS
B_DV = D_MODEL // B_HEADS
B_GATE_RANK = 16
B_GATE_TAU = 16.0
B_CHUNK = 64

C_HEADS = 8
C_DK = D_MODEL // C_HEADS
C_DV = D_MODEL // C_HEADS
C_CONV = 4
C_CHUNK = 64

D_FF = 2816
FFN_CONV = 3

LANES = 128
SUBLANES = 8
ROW_TILE = 256
FFN_COL_CHUNK = 1408
SAMPLE_PAGES_PER_STEP = 8
VMEM_LIMIT = 56 * 1024 * 1024

NT_DIMS = (((1,), (1,)), ((), ()))
TN_DIMS = (((0,), (0,)), ((), ()))


def _params(sem, vmem=VMEM_LIMIT):
    return pltpu.CompilerParams(dimension_semantics=sem, vmem_limit_bytes=vmem)


def _const_spec(shape):
    nd = len(shape)
    return pl.BlockSpec(shape, lambda *_: (0,) * nd, pipeline_mode=pl.Buffered(1))


def _rms(x, g):
    return x * lax.rsqrt(jnp.mean(x * x, axis=-1, keepdims=True) + NORM_EPS) * g


def _sigmoid(x):
    return 1.0 / (1.0 + jnp.exp(-x))


def _silu(x):
    return x * _sigmoid(x)


def _softplus(x):
    return jnp.maximum(x, 0.0) + jnp.log(1.0 + jnp.exp(-jnp.abs(x)))


def _bdot(a, b):
    return jnp.dot(a.astype(BF16), b.astype(BF16), preferred_element_type=F32)


def _bdot_nt(a, b):
    return lax.dot_general(a.astype(BF16), b.astype(BF16), NT_DIMS, preferred_element_type=F32)


def _bdot_tn(a, b):
    return lax.dot_general(a.astype(BF16), b.astype(BF16), TN_DIMS, preferred_element_type=F32)


def _cumsum_rows(x):
    n = x.shape[0]
    row = lax.broadcasted_iota(jnp.int32, x.shape, 0)
    s = 1
    while s < n:
        x = x + jnp.where(row >= s, pltpu.roll(x, s, 0), 0.0)
        s *= 2
    return x


def _shift_rows(u, tail, s):
    r = pltpu.roll(u, s, 0)
    row = lax.broadcasted_iota(jnp.int32, tail.shape, 0)
    head = jnp.where(row < s, pltpu.roll(tail, s, 0), r[0:SUBLANES])
    return jnp.concatenate([head, r[SUBLANES:]], axis=0)


def _norm_proj_kernel(x_ref, g_ref, w_ref, *out_refs, splits):
    hb = _rms(x_ref[...], g_ref[...]).astype(BF16)
    for o_ref, (c0, c1) in zip(out_refs, splits):
        o_ref[...] = jnp.dot(hb, w_ref[:, c0:c1], preferred_element_type=F32).astype(o_ref.dtype)


def _norm_proj(x2d, g, w, splits):
    n = x2d.shape[0]
    tm = min(ROW_TILE, n)
    out_shape = [jax.ShapeDtypeStruct((n, c1 - c0), F32) for c0, c1 in splits]
    out_specs = [pl.BlockSpec((tm, c1 - c0), lambda i: (i, 0)) for c0, c1 in splits]
    return pl.pallas_call(
        functools.partial(_norm_proj_kernel, splits=splits),
        grid=(n // tm,),
        in_specs=[pl.BlockSpec((tm, D_MODEL), lambda i: (i, 0)),
                  _const_spec((1, D_MODEL)),
                  _const_spec(w.shape)],
        out_specs=out_specs,
        out_shape=out_shape,
        compiler_params=_params(("parallel",)),
        name="norm_proj",
    )(x2d, g.reshape(1, D_MODEL), w)


def _final_norm_kernel(x_ref, g_ref, o_ref):
    o_ref[...] = _rms(x_ref[...], g_ref[...])


def _final_norm(x2d, g):
    n = x2d.shape[0]
    tm = min(2 * ROW_TILE, n)
    return pl.pallas_call(
        _final_norm_kernel,
        grid=(n // tm,),
        in_specs=[pl.BlockSpec((tm, D_MODEL), lambda i: (i, 0)), _const_spec((1, D_MODEL))],
        out_specs=pl.BlockSpec((tm, D_MODEL), lambda i: (i, 0)),
        out_shape=jax.ShapeDtypeStruct((n, D_MODEL), F32),
        compiler_params=_params(("parallel",)),
        name="final_norm",
    )(x2d, g.reshape(1, D_MODEL))


def _post_kernel(*refs, per_seq_prev, tiles_per_seq):
    if per_seq_prev:
        (x_ref, o_ref, wo_ref, g_ref, win_ref, cw_ref, cb_ref, wout_ref, p1_ref, p2_ref,
         y_ref, u_ref) = refs
    else:
        (x_ref, o_ref, wo_ref, g_ref, win_ref, cw_ref, cb_ref, wout_ref,
         y_ref, tail_ref, carry_scr) = refs

        @pl.when(pl.program_id(0) % tiles_per_seq == 0)
        def _():
            carry_scr[...] = jnp.zeros_like(carry_scr)

    tm = x_ref.shape[0]
    x = x_ref[...] + jnp.dot(o_ref[...].astype(BF16), wo_ref[...], preferred_element_type=F32)
    hb = _rms(x, g_ref[...]).astype(BF16)
    if per_seq_prev:
        t_in_seq = lax.broadcasted_iota(jnp.int32, (tm, 1), 0) % SUBLANES
    acc = x
    for c in range(D_FF // FFN_COL_CHUNK):
        conv = []
        for off in (c * FFN_COL_CHUNK, D_FF + c * FFN_COL_CHUNK):
            cols = slice(off, off + FFN_COL_CHUNK)
            u = jnp.dot(hb, win_ref[:, cols], preferred_element_type=F32)
            if per_seq_prev:
                u1 = jnp.where(t_in_seq < 1, p1_ref[:, cols], pltpu.roll(u, 1, 0))
                u2 = jnp.where(t_in_seq < 2, p2_ref[:, cols], pltpu.roll(u, 2, 0))
                u_ref[:, cols] = u
            else:
                tail = carry_scr[:, cols]
                u1 = _shift_rows(u, tail, 1)
                u2 = _shift_rows(u, tail, 2)
                carry_scr[:, cols] = u[tm - SUBLANES:tm]
                tail_ref[:, cols] = u[tm - SUBLANES:tm]
            conv.append(u2 * cw_ref[0:1, cols] + u1 * cw_ref[1:2, cols] + u * cw_ref[2:3, cols]
                        + cb_ref[:, cols])
        act = (_silu(conv[0]) * conv[1]).astype(BF16)
        acc = acc + jnp.dot(act, wout_ref[c * FFN_COL_CHUNK:(c + 1) * FFN_COL_CHUNK, :],
                            preferred_element_type=F32)
    y_ref[...] = acc


def _post(x2d, o2d, w_o, g, w_in, conv_w, conv_b, w_out, seq_len, prev=None):
    n = x2d.shape[0]
    tm = min(ROW_TILE, n)
    nseq = n // seq_len
    row_spec = pl.BlockSpec((tm, D_MODEL), lambda i: (i, 0))
    in_specs = [row_spec, row_spec,
                _const_spec((D_MODEL, D_MODEL)), _const_spec((1, D_MODEL)),
                _const_spec((D_MODEL, 2 * D_FF)), _const_spec((FFN_CONV, 2 * D_FF)),
                _const_spec((1, 2 * D_FF)), _const_spec((D_FF, D_MODEL))]
    args = [x2d, o2d, w_o, g.reshape(1, D_MODEL), w_in, conv_w, conv_b.reshape(1, 2 * D_FF), w_out]
    y_shape = jax.ShapeDtypeStruct((n, D_MODEL), F32)
    if prev is None:
        assert seq_len % tm == 0
        tps = seq_len // tm
        y, tail = pl.pallas_call(
            functools.partial(_post_kernel, per_seq_prev=False, tiles_per_seq=tps),
            grid=(n // tm,),
            in_specs=in_specs,
            out_specs=[row_spec, pl.BlockSpec((None, SUBLANES, 2 * D_FF), lambda i: (i // tps, 0, 0))],
            out_shape=[y_shape, jax.ShapeDtypeStruct((nseq, SUBLANES, 2 * D_FF), F32)],
            scratch_shapes=[pltpu.VMEM((SUBLANES, 2 * D_FF), F32)],
            compiler_params=_params(("arbitrary",)),
            name="post_ffn_prompt",
        )(*args)
        return y, tail[:, SUBLANES - (FFN_CONV - 1):]
    assert seq_len == SUBLANES and n == tm
    pad = ((0, 0), (0, SUBLANES - 1), (0, 0))
    p1 = jnp.pad(prev[:, 1:2], pad).reshape(n, 2 * D_FF)
    p2 = jnp.pad(prev, ((0, 0), (0, SUBLANES - 2), (0, 0))).reshape(n, 2 * D_FF)
    wide_spec = pl.BlockSpec((tm, 2 * D_FF), lambda i: (i, 0))
    y, u = pl.pallas_call(
        functools.partial(_post_kernel, per_seq_prev=True, tiles_per_seq=1),
        grid=(1,),
        in_specs=in_specs + [wide_spec, wide_spec],
        out_specs=[row_spec, wide_spec],
        out_shape=[y_shape, jax.ShapeDtypeStruct((n, 2 * D_FF), F32)],
        compiler_params=_params(("arbitrary",)),
        name="post_ffn_sample",
    )(*args, p1, p2)
    return y, u.reshape(nseq, seq_len, 2 * D_FF)[:, seq_len - (FFN_CONV - 1):]


def _topk_bias_t(gate_t, n_valid, n_rows):
    row = lax.broadcasted_iota(jnp.int32, gate_t.shape, 0)
    valid = row < n_valid
    out = []
    for n in range(n_rows):
        gn = gate_t[n:n + 1, :]
        beats = jnp.where(gate_t > gn, 1.0, jnp.where((gate_t == gn) & (row < n), 1.0, 0.0))
        rank = jnp.sum(jnp.where(valid, beats, 0.0), axis=0, keepdims=True)
        out.append(jnp.where(rank < MOBA_TOPK, 0.0, NEG_INF))
    return out


def _moba_prompt_tile(c, q_ref, o_ref, kb_scr, vt_scr, km_scr):
    blk = MOBA_BLOCK
    hd = A_HEAD_DIM
    q = q_ref[...]
    lane = lax.broadcasted_iota(jnp.int32, (1, LANES), 1)
    causal = (lax.broadcasted_iota(jnp.int32, (blk, blk), 0)
              <= lax.broadcasted_iota(jnp.int32, (blk, blk), 1))
    pair = range(2)
    qh = [jnp.where(lane // hd == hh, q, 0.0) for hh in pair]
    qs = [(x * (hd ** -0.5)).astype(BF16) for x in qh]
    s = [[lax.dot_general(kb_scr[n], qs[hh], NT_DIMS, preferred_element_type=F32) for n in range(c + 1)]
         for hh in pair]
    for hh in pair:
        s[hh][c] = jnp.where(causal, s[hh][c], NEG_INF)
        if c > MOBA_TOPK:
            gate_t = lax.dot_general(km_scr[...], qh[hh], NT_DIMS, precision=HIGHEST,
                                     preferred_element_type=F32)
            for n, b in enumerate(_topk_bias_t(gate_t, c, c)):
                s[hh][n] = s[hh][n] + b
    m = []
    for hh in pair:
        mv = s[hh][c]
        for n in range(c):
            mv = jnp.maximum(mv, s[hh][n])
        m.append(jnp.max(mv, axis=0, keepdims=True))
    o_t = []
    for hh in pair:
        lv = None
        acc = None
        for n in range(c + 1):
            p = jnp.exp(s[hh][n] - m[hh])
            lv = p if lv is None else lv + p
            pv = jnp.dot(vt_scr[n, hh * hd:(hh + 1) * hd, :], p.astype(BF16),
                         preferred_element_type=F32)
            acc = pv if acc is None else acc + pv
        o_t.append(acc / jnp.sum(lv, axis=0, keepdims=True))
    o_ref[...] = jnp.concatenate(o_t, axis=0).T.astype(o_ref.dtype)


def _moba_prompt_kernel(q_ref, k_ref, v_ref, o_ref, kb_scr, vt_scr, km_scr, *, nb):
    i = pl.program_id(2)
    blk = MOBA_BLOCK

    @pl.when(i == 0)
    def _():
        km_scr[...] = jnp.zeros_like(km_scr)
        for n in range(nb):
            kn = k_ref[n * blk:(n + 1) * blk, :]
            kb_scr[n] = kn.astype(BF16)
            vt_scr[n] = v_ref[n * blk:(n + 1) * blk, :].T.astype(BF16)
            km_scr[n:n + 1, :] = jnp.mean(kn, axis=0, keepdims=True)

    for c in range(nb):
        pl.when(i == c)(functools.partial(_moba_prompt_tile, c, q_ref, o_ref, kb_scr, vt_scr, km_scr))


def _moba_prompt_attn(q, k, v):
    bsz, s, _ = q.shape
    assert s % MOBA_BLOCK == 0
    nb = s // MOBA_BLOCK
    nb_pad = -(-nb // SUBLANES) * SUBLANES
    hp = D_MODEL // LANES
    q_spec = pl.BlockSpec((None, MOBA_BLOCK, LANES), lambda b, h, i: (b, i, h))
    kv_spec = pl.BlockSpec((None, s, LANES), lambda b, h, i: (b, 0, h))
    return pl.pallas_call(
        functools.partial(_moba_prompt_kernel, nb=nb),
        grid=(bsz, hp, nb),
        in_specs=[q_spec, kv_spec, kv_spec],
        out_specs=q_spec,
        out_shape=jax.ShapeDtypeStruct((bsz, s, D_MODEL), BF16),
        scratch_shapes=[pltpu.VMEM((nb, MOBA_BLOCK, LANES), BF16),
                        pltpu.VMEM((nb, LANES, MOBA_BLOCK), BF16),
                        pltpu.VMEM((nb_pad, LANES), F32)],
        compiler_params=_params(("parallel", "parallel", "arbitrary")),
        name="moba_prompt_attn",
    )(q, k, v)


def _topk_select(gate, n_valid, n_cols):
    col = lax.broadcasted_iota(jnp.int32, gate.shape, 1)
    valid = col < n_valid
    sel = []
    for n in range(n_cols):
        gn = gate[:, n:n + 1]
        beats = jnp.where(gate > gn, 1.0, jnp.where((gate == gn) & (col < n), 1.0, 0.0))
        rank = jnp.sum(jnp.where(valid, beats, 0.0), axis=1, keepdims=True)
        sel.append(rank < MOBA_TOPK)
    return sel


def _moba_sample_kernel(pt_ref, q_ref, kn_ref, vn_ref, *refs, n_pages, t_new):
    g_pages = SAMPLE_PAGES_PER_STEP
    k_refs = refs[:g_pages]
    v_refs = refs[g_pages:2 * g_pages]
    o_ref = refs[2 * g_pages]
    qbf_scr, new_scr, gate_scr, sc_scr, l_scr, acc_scr = refs[2 * g_pages + 1:]
    del pt_ref
    j = pl.program_id(1)
    n_steps = n_pages // g_pages
    n_blocks = n_pages * PAGE_SIZE // MOBA_BLOCK
    pages_per_block = MOBA_BLOCK // PAGE_SIZE
    blocks_per_step = g_pages // pages_per_block
    rows = A_HEADS * t_new
    head_of_row = lax.broadcasted_iota(jnp.int32, (rows, D_MODEL), 0) // t_new
    head_of_lane = lax.broadcasted_iota(jnp.int32, (rows, D_MODEL), 1) // A_HEAD_DIM

    @pl.when(j == 0)
    def _():
        qbd = jnp.where(head_of_row == head_of_lane,
                        jnp.concatenate([q_ref[...]] * A_HEADS, axis=0), 0.0)
        qbf_scr[...] = (qbd * (A_HEAD_DIM ** -0.5)).astype(BF16)
        gate_scr[...] = jnp.zeros_like(gate_scr)

    @pl.when(j < n_steps)
    def _():
        col = lax.broadcasted_iota(jnp.int32, gate_scr.shape, 1)
        gate = gate_scr[...]
        for bb in range(blocks_per_step):
            ssum = None
            for g in range(bb * pages_per_block, (bb + 1) * pages_per_block):
                s = jnp.dot(qbf_scr[...], k_refs[g][...].astype(BF16), preferred_element_type=F32)
                sc_scr[j * g_pages + g] = s
                ssum = s if ssum is None else ssum + s
            gmean = jnp.sum(ssum, axis=1, keepdims=True) * (1.0 / MOBA_BLOCK)
            gate = jnp.where(col == j * blocks_per_step + bb, gmean, gate)
        gate_scr[...] = gate

    @pl.when(j == n_steps - 1)
    def _():
        sel = _topk_select(gate_scr[...], n_blocks, n_blocks)
        new_scr[...] = jnp.zeros_like(new_scr)
        new_scr[0, 0:t_new, :] = kn_ref[...].astype(BF16)
        new_scr[1, 0:t_new, :] = vn_ref[...].astype(BF16)
        s_own = lax.dot_general(qbf_scr[...], new_scr[0], NT_DIMS, preferred_element_type=F32)
        r_tok = lax.broadcasted_iota(jnp.int32, s_own.shape, 0) % t_new
        c_tok = lax.broadcasted_iota(jnp.int32, s_own.shape, 1)
        s_own = jnp.where(c_tok <= r_tok, s_own, NEG_INF)
        mv = s_own
        for b in range(n_blocks):
            bias = jnp.broadcast_to(jnp.where(sel[b], 0.0, NEG_INF), s_own.shape)
            for p in range(b * pages_per_block, (b + 1) * pages_per_block):
                mv = jnp.maximum(mv, sc_scr[p] + bias)
        m = jnp.max(mv, axis=1, keepdims=True)
        p_own = jnp.exp(s_own - m)
        lv = p_own
        for b in range(n_blocks):
            shift = jnp.broadcast_to(jnp.where(sel[b], 0.0, NEG_INF) - m, s_own.shape)
            for p in range(b * pages_per_block, (b + 1) * pages_per_block):
                e = jnp.exp(sc_scr[p] + shift)
                lv = lv + e
                sc_scr[p] = e
        l_scr[...] = jnp.sum(lv, axis=1, keepdims=True)
        acc_scr[...] = jnp.dot(p_own.astype(BF16), new_scr[1], preferred_element_type=F32)

    @pl.when(j >= n_steps)
    def _():
        acc = acc_scr[...]
        for g in range(g_pages):
            pp = sc_scr[(j - n_steps) * g_pages + g].astype(BF16)
            acc = acc + lax.dot_general(pp, v_refs[g][...].astype(BF16), NT_DIMS,
                                        preferred_element_type=F32)
        acc_scr[...] = acc

    @pl.when(j == 2 * n_steps - 1)
    def _():
        acc = jnp.where(head_of_row == head_of_lane, acc_scr[...] / l_scr[...], 0.0)
        o = acc[0:t_new]
        for h in range(1, A_HEADS):
            o = o + acc[h * t_new:(h + 1) * t_new]
        o_ref[...] = o


def _moba_sample_attn(q, k_new, v_new, cache_kt, cache_vt, page_table, layer):
    bsz, t_new, _ = q.shape
    n_pages = page_table.shape[1]
    g_pages = SAMPLE_PAGES_PER_STEP
    assert (n_pages * PAGE_SIZE) % MOBA_BLOCK == 0 and MOBA_BLOCK % PAGE_SIZE == 0
    assert (g_pages * PAGE_SIZE) % MOBA_BLOCK == 0 and n_pages % g_pages == 0
    assert t_new == SUBLANES and A_HEADS * t_new == LANES and PAGE_SIZE == LANES
    n_steps = n_pages // g_pages
    n_blocks = n_pages * PAGE_SIZE // MOBA_BLOCK
    assert MOBA_TOPK <= n_blocks <= LANES
    tok_spec = pl.BlockSpec((None, t_new, D_MODEL), lambda b, j, pt: (b, 0, 0))

    def k_spec(g):
        return pl.BlockSpec(
            (None, None, D_MODEL, PAGE_SIZE),
            lambda b, j, pt: (layer, pt[b, g_pages * jnp.minimum(j, n_steps - 1) + g], 0, 0))

    def v_spec(g):
        return pl.BlockSpec(
            (None, None, D_MODEL, PAGE_SIZE),
            lambda b, j, pt: (layer, pt[b, g_pages * jnp.maximum(j - n_steps, 0) + g], 0, 0))

    rows = A_HEADS * t_new
    grid_spec = pltpu.PrefetchScalarGridSpec(
        num_scalar_prefetch=1,
        grid=(bsz, 2 * n_steps),
        in_specs=[tok_spec, tok_spec, tok_spec] + [k_spec(g) for g in range(g_pages)]
        + [v_spec(g) for g in range(g_pages)],
        out_specs=tok_spec,
        scratch_shapes=[pltpu.VMEM((rows, D_MODEL), BF16),
                        pltpu.VMEM((2, LANES, D_MODEL), BF16), pltpu.VMEM((rows, LANES), F32),
                        pltpu.VMEM((n_pages, rows, PAGE_SIZE), F32), pltpu.VMEM((rows, 1), F32),
                        pltpu.VMEM((rows, D_MODEL), F32)],
    )
    return pl.pallas_call(
        functools.partial(_moba_sample_kernel, n_pages=n_pages, t_new=t_new),
        grid_spec=grid_spec,
        out_shape=jax.ShapeDtypeStruct((bsz, t_new, D_MODEL), F32),
        compiler_params=_params(("parallel", "arbitrary")),
        name="moba_sample_attn",
    )(page_table, q, k_new, v_new, *([cache_kt] * g_pages), *([cache_vt] * g_pages))


def _gla_kernel(q_ref, k_ref, v_ref, r_ref, glr_ref, wg_ref, bg_ref, ng_ref, s0_ref,
                o_ref, sout_ref, s_scr, *, valid_rows):
    ci = pl.program_id(1)
    c = q_ref.shape[0]

    @pl.when(ci == 0)
    def _():
        s_scr[...] = s0_ref[...]

    x = jnp.dot(glr_ref[...].astype(BF16), wg_ref[...], preferred_element_type=F32) + bg_ref[...]
    gk = -_softplus(-x) * (1.0 / B_GATE_TAU)
    if valid_rows < c:
        gk = jnp.where(lax.broadcasted_iota(jnp.int32, gk.shape, 0) < valid_rows, gk, 0.0)
    b = _cumsum_rows(gk)
    b_last = b[c - 1:c, :]
    k = k_ref[...]
    q_t = (q_ref[...] * (B_DK ** -0.5) * jnp.exp(b)).astype(BF16)
    k_t = (k * jnp.exp(-b)).astype(BF16)
    k_end = (k * jnp.exp(b_last - b)).astype(BF16)
    e_last = jnp.exp(b_last)
    causal = (lax.broadcasted_iota(jnp.int32, (c, c), 1) <= lax.broadcasted_iota(jnp.int32, (c, c), 0))
    for h in range(B_HEADS):
        kc = slice(h * B_DK, (h + 1) * B_DK)
        vc = slice(h * B_DV, (h + 1) * B_DV)
        vh = v_ref[:, vc].astype(BF16)
        st = s_scr[h]
        att = jnp.where(causal, lax.dot_general(q_t[:, kc], k_t[:, kc], NT_DIMS,
                                                preferred_element_type=F32), 0.0)
        o = (jnp.dot(att.astype(BF16), vh, preferred_element_type=F32)
             + lax.dot_general(q_t[:, kc], st.astype(BF16), NT_DIMS, preferred_element_type=F32))
        s_scr[h] = st * e_last[:, kc] + lax.dot_general(vh, k_end[:, kc], TN_DIMS,
                                                        preferred_element_type=F32)
        o = o * lax.rsqrt(jnp.mean(o * o, axis=-1, keepdims=True) + NORM_EPS) * ng_ref[...]
        o_ref[:, vc] = (o * _silu(r_ref[:, vc])).astype(o_ref.dtype)

    @pl.when(ci == pl.num_programs(1) - 1)
    def _():
        sout_ref[...] = s_scr[...]


def _gla_scan(q, k, v, r, glr, w_gate, b_gate, norm_g, s0_t, valid_rows):
    bsz, t, _ = q.shape
    c = B_CHUNK
    assert t % c == 0
    nk = B_HEADS * B_DK

    def tok(width):
        return pl.BlockSpec((None, c, width), lambda b, ci: (b, ci, 0))

    st_spec = pl.BlockSpec((None, B_HEADS, B_DV, B_DK), lambda b, ci: (b, 0, 0, 0))
    return pl.pallas_call(
        functools.partial(_gla_kernel, valid_rows=valid_rows),
        grid=(bsz, t // c),
        in_specs=[tok(nk), tok(nk), tok(D_MODEL), tok(D_MODEL), tok(LANES),
                  _const_spec((LANES, nk)), _const_spec((1, nk)), _const_spec((1, B_DV)), st_spec],
        out_specs=[tok(D_MODEL), st_spec],
        out_shape=[jax.ShapeDtypeStruct((bsz, t, D_MODEL), BF16),
                   jax.ShapeDtypeStruct((bsz, B_HEADS, B_DV, B_DK), F32)],
        scratch_shapes=[pltpu.VMEM((B_HEADS, B_DV, B_DK), F32)],
        compiler_params=_params(("parallel", "arbitrary")),
        name="gla_scan",
    )(q, k, v, r, glr, w_gate, b_gate.reshape(1, nk), norm_g.reshape(1, B_DV), s0_t)


def _split_bf16(x):
    hi = x.astype(BF16)
    return hi, (x - hi.astype(F32)).astype(BF16)


def _dot3(a, b, dims=(((1,), (0,)), ((), ()))):
    ah, al = _split_bf16(a)
    bh, bl = _split_bf16(b)

    def f(x, y):
        return lax.dot_general(x, y, dims, preferred_element_type=F32)
    return f(ah, bh) + (f(ah, bl) + f(al, bh))


def _unit_lower_inverse_minus_eye(a_list):
    c = a_list[0].shape[0]
    x = [-a for a in a_list]
    n = list(x)
    p = 2
    while p < c:
        x = [_dot3(xi, xi) for xi in x]
        n = [ni + xi + _dot3(ni, xi) for ni, xi in zip(n, x)]
        p *= 2
    return n


def _gdn_kernel(qkv_ref, z_ref, ba_ref, cw_ref, alog_ref, dtb_ref, ng_ref, s0_ref,
                o_ref, sout_ref, s_scr, carry_scr, *, valid_lo, valid_hi):
    ci = pl.program_id(1)
    c = qkv_ref.shape[0]
    heads = range(C_HEADS)

    @pl.when(ci == 0)
    def _():
        s_scr[...] = s0_ref[...]
        carry_scr[...] = jnp.zeros_like(carry_scr)

    u = qkv_ref[...]
    tail = carry_scr[...]
    conv = u * cw_ref[C_CONV - 1:C_CONV, :]
    for s in range(1, C_CONV):
        conv = conv + _shift_rows(u, tail, s) * cw_ref[C_CONV - 1 - s:C_CONV - s, :]
    carry_scr[...] = u[c - SUBLANES:c]
    xc = _silu(conv)

    ba = ba_ref[...]
    beta_all = _sigmoid(ba)
    g_all = -jnp.exp(alog_ref[...]) * _softplus(ba + dtb_ref[...])
    if valid_lo > 0 or valid_hi < c:
        rowi = lax.broadcasted_iota(jnp.int32, ba.shape, 0)
        ok = (rowi >= valid_lo) & (rowi < valid_hi)
        beta_all = jnp.where(ok, beta_all, 0.0)
        g_all = jnp.where(ok, g_all, 0.0)
    gam_all = _cumsum_rows(g_all)
    gam_t = jnp.concatenate([gam_all, jnp.zeros((LANES - c, LANES), F32)], axis=0).T
    ri = lax.broadcasted_iota(jnp.int32, (c, c), 0)
    cj = lax.broadcasted_iota(jnp.int32, (c, c), 1)
    incl = cj <= ri
    strict = cj < ri

    qn, kn, gcol, dec, rhs, a_mat = [], [], [], [], [], []
    for h in heads:
        qh = xc[:, h * C_DK:(h + 1) * C_DK]
        kh = xc[:, D_MODEL + h * C_DK:D_MODEL + (h + 1) * C_DK]
        vh = xc[:, 2 * D_MODEL + h * C_DV:2 * D_MODEL + (h + 1) * C_DV]
        qn.append(qh * lax.rsqrt(jnp.sum(qh * qh, axis=-1, keepdims=True) + NORM_EPS) * (C_DK ** -0.5))
        kn.append(kh * lax.rsqrt(jnp.sum(kh * kh, axis=-1, keepdims=True) + NORM_EPS))
        beta = beta_all[:, h:h + 1]
        gcol.append(gam_all[:, C_HEADS + h:C_HEADS + h + 1])
        grow = gam_t[C_HEADS + h:C_HEADS + h + 1, 0:c]
        dec.append(jnp.where(incl, jnp.exp(jnp.where(incl, gcol[h] - grow, 0.0)), 0.0))
        kb = kn[h] * beta
        rhs.append(jnp.concatenate([kb * jnp.exp(gcol[h]), vh * beta], axis=1))
        a_mat.append(jnp.where(strict, _dot3(kb, kn[h], NT_DIMS) * dec[h], 0.0))
    inv_m = _unit_lower_inverse_minus_eye(a_mat)
    wu = [rhs[h] + _dot3(inv_m[h], rhs[h]) for h in heads]
    att = [_bdot_nt(qn[h], kn[h]) * dec[h] for h in heads]

    st = [s_scr[h] for h in heads]
    stb = [s.astype(BF16) for s in st]
    v_new = [wu[h][:, C_DK:] - lax.dot_general(wu[h][:, :C_DK].astype(BF16), stb[h], NT_DIMS,
                                               preferred_element_type=F32) for h in heads]
    o = [lax.dot_general((qn[h] * jnp.exp(gcol[h])).astype(BF16), stb[h], NT_DIMS,
                         preferred_element_type=F32) + _bdot(att[h], v_new[h]) for h in heads]
    for h in heads:
        g_last = gcol[h][c - 1:c, :]
        s_scr[h] = jnp.exp(g_last) * st[h] + _bdot_tn(v_new[h], kn[h] * jnp.exp(g_last - gcol[h]))
    for h in heads:
        hc = slice(h * C_DV, (h + 1) * C_DV)
        oh = o[h] * lax.rsqrt(jnp.mean(o[h] * o[h], axis=-1, keepdims=True) + NORM_EPS) * ng_ref[...]
        o_ref[:, hc] = (oh * _silu(z_ref[:, hc])).astype(o_ref.dtype)

    @pl.when(ci == pl.num_programs(1) - 1)
    def _():
        sout_ref[...] = s_scr[...]


def _gdn_scan(qkv, z, ba, conv_w, a_log, dt_bias, norm_g, s0_t, valid_lo, valid_hi):
    bsz, t, _ = qkv.shape
    c = C_CHUNK
    assert t % c == 0

    def tok(width):
        return pl.BlockSpec((None, c, width), lambda b, ci: (b, ci, 0))

    def under_a(p):
        return jnp.zeros((1, LANES), F32).at[0, C_HEADS:2 * C_HEADS].set(p)

    st_spec = pl.BlockSpec((None, C_HEADS, C_DV, C_DK), lambda b, ci: (b, 0, 0, 0))
    return pl.pallas_call(
        functools.partial(_gdn_kernel, valid_lo=valid_lo, valid_hi=valid_hi),
        grid=(bsz, t // c),
        in_specs=[tok(3 * D_MODEL), tok(D_MODEL), tok(LANES),
                  _const_spec((C_CONV, 3 * D_MODEL)), _const_spec((1, LANES)),
                  _const_spec((1, LANES)), _const_spec((1, C_DV)), st_spec],
        out_specs=[tok(D_MODEL), st_spec],
        out_shape=[jax.ShapeDtypeStruct((bsz, t, D_MODEL), BF16),
                   jax.ShapeDtypeStruct((bsz, C_HEADS, C_DV, C_DK), F32)],
        scratch_shapes=[pltpu.VMEM((C_HEADS, C_DV, C_DK), F32),
                        pltpu.VMEM((SUBLANES, 3 * D_MODEL), F32)],
        compiler_params=_params(("parallel", "arbitrary")),
        name="gdn_scan",
    )(qkv, z, ba, conv_w, under_a(a_log), under_a(dt_bias), norm_g.reshape(1, C_DV), s0_t)


def _pad_cols(w, width):
    return jnp.pad(w, ((0, 0), (0, width - w.shape[1])))


def _pad_tokens(a, before, total):
    return jnp.pad(a, ((0, 0), (before, total - before - a.shape[1]), (0, 0)))


def _moba_mixer(x, g, w_in, cache=None):
    bsz, t, _ = x.shape
    splits = tuple((i * D_MODEL, (i + 1) * D_MODEL) for i in range(3))
    q, k, v = (a.reshape(bsz, t, D_MODEL)
               for a in _norm_proj(x.reshape(bsz * t, D_MODEL), g, w_in.astype(BF16), splits))
    if cache is None:
        o = _moba_prompt_attn(q, k, v)
    else:
        o = _moba_sample_attn(q, k, v, *cache)
    rows = (bsz, t, A_HEADS, A_HEAD_DIM)
    return o.reshape(bsz * t, D_MODEL), k.reshape(rows), v.reshape(rows)


def _gla_mixer(x, g, w_in, w_gate, b_gate, norm_g, s0):
    bsz, t, _ = x.shape
    nk = B_HEADS * B_DK
    edges = (0, nk, 2 * nk, 2 * nk + D_MODEL, 2 * nk + 2 * D_MODEL, 2 * nk + 2 * D_MODEL + LANES)
    splits = tuple(zip(edges[:-1], edges[1:]))
    parts = _norm_proj(x.reshape(bsz * t, D_MODEL), g, _pad_cols(w_in, edges[-1]).astype(BF16), splits)
    t_pad = -(-t // B_CHUNK) * B_CHUNK
    q, k, v, r, glr = (_pad_tokens(a.reshape(bsz, t, -1), 0, t_pad) for a in parts)
    wg = jnp.pad(w_gate, ((0, LANES - B_GATE_RANK), (0, 0))).astype(BF16)
    if s0 is None:
        s0_t = jnp.zeros((bsz, B_HEADS, B_DV, B_DK), F32)
    else:
        s0_t = jnp.swapaxes(s0, -1, -2)
    o, s_t = _gla_scan(q, k, v, r, glr, wg, b_gate, norm_g, s0_t, min(t, B_CHUNK))
    return o[:, :t].reshape(bsz * t, D_MODEL), jnp.swapaxes(s_t, -1, -2)


def _gdn_mixer(x, g, w_in, conv_w, a_log, dt_bias, norm_g, s0, conv_prev):
    bsz, t, _ = x.shape
    edges = (0, 3 * D_MODEL, 4 * D_MODEL, 4 * D_MODEL + LANES)
    splits = tuple(zip(edges[:-1], edges[1:]))
    parts = _norm_proj(x.reshape(bsz * t, D_MODEL), g, _pad_cols(w_in, edges[-1]).astype(BF16), splits)
    qkv, z, ba = (a.reshape(bsz, t, -1) for a in parts)
    hist = C_CONV - 1
    if conv_prev is None:
        lo = 0
        conv_new = qkv[:, t - hist:]
        s0_t = jnp.zeros((bsz, C_HEADS, C_DV, C_DK), F32)
        qkv_in = qkv
    else:
        lo = hist
        assert t >= hist
        conv_new = qkv[:, t - hist:]
        s0_t = jnp.swapaxes(s0, -1, -2)
        qkv_in = jnp.concatenate([conv_prev, qkv], axis=1)
    t_pad = -(-(lo + t) // C_CHUNK) * C_CHUNK
    if t_pad != lo + t or lo:
        assert t_pad == C_CHUNK
    qkv_in = _pad_tokens(qkv_in, 0, t_pad)
    z = _pad_tokens(z, lo, t_pad)
    ba = _pad_tokens(ba, lo, t_pad)
    o, s_t = _gdn_scan(qkv_in, z, ba, conv_w, a_log, dt_bias, norm_g, s0_t, lo, min(lo + t, C_CHUNK))
    return o[:, lo:lo + t].reshape(bsz * t, D_MODEL), jnp.swapaxes(s_t, -1, -2), conv_new


def _run_group(x, cache, state_gla, state_gdn, state_gdn_conv, state_ffn_conv, w):
    bsz, t, _ = x.shape
    x2d = x.reshape(bsz * t, D_MODEL)
    out = {name: [] for name in ("k", "v", "gla", "gdn", "gconv", "fconv")}
    for i in range(DEPTH):
        j = i // N_MIXERS
        xin = x2d.reshape(bsz, t, D_MODEL)
        if i % N_MIXERS == 0:
            layer_cache = None if cache is None else (cache[0], cache[1], cache[2], j)
            o, kr, vr = _moba_mixer(xin, w["norm_mix"][i], w["moba_w_in"][j], layer_cache)
            out["k"].append(kr)
            out["v"].append(vr)
            w_o = w["moba_w_out"][j]
        elif i % N_MIXERS == 1:
            s0 = None if state_gla is None else state_gla[j]
            o, s = _gla_mixer(xin, w["norm_mix"][i], w["gla_w_in"][j], w["gla_w_gate"][j],
                              w["gla_b_gate"][j], w["gla_norm"][j], s0)
            out["gla"].append(s)
            w_o = w["gla_w_out"][j]
        else:
            s0 = None if state_gdn is None else state_gdn[j]
            cp = None if state_gdn_conv is None else state_gdn_conv[j]
            o, s, cn = _gdn_mixer(xin, w["norm_mix"][i], w["gdn_w_in"][j], w["gdn_conv_w"][j],
                                  w["gdn_a_log"][j], w["gdn_dt_bias"][j], w["gdn_norm"][j], s0, cp)
            out["gdn"].append(s)
            out["gconv"].append(cn)
            w_o = w["gdn_w_out"][j]
        prev = None if state_ffn_conv is None else state_ffn_conv[i]
        x2d, fc = _post(x2d, o, w_o.astype(BF16), w["norm_ffn"][i], w["ffn_w_in"][i].astype(BF16),
                        w["ffn_conv_w"][i], w["ffn_conv_b"][i], w["ffn_w_out"][i].astype(BF16), t, prev)
        out["fconv"].append(fc)
    y = _final_norm(x2d, w["norm_final"]).reshape(bsz, t, D_MODEL)
    return y, {name: jnp.stack(v) for name, v in out.items()}


def kernel(x_prompt, x_sample, cache_k, cache_v, page_table, state_gla, state_gdn, state_gdn_conv,
           state_ffn_conv, norm_mix, norm_ffn, norm_final, moba_w_in, moba_w_out, gla_w_in, gla_w_gate,
           gla_b_gate, gla_norm, gla_w_out, gdn_w_in, gdn_conv_w, gdn_a_log, gdn_dt_bias, gdn_norm,
           gdn_w_out, ffn_w_in, ffn_conv_w, ffn_conv_b, ffn_w_out):
    w = dict(norm_mix=norm_mix, norm_ffn=norm_ffn, norm_final=norm_final, moba_w_in=moba_w_in,
             moba_w_out=moba_w_out, gla_w_in=gla_w_in, gla_w_gate=gla_w_gate, gla_b_gate=gla_b_gate,
             gla_norm=gla_norm, gla_w_out=gla_w_out, gdn_w_in=gdn_w_in, gdn_conv_w=gdn_conv_w,
             gdn_a_log=gdn_a_log, gdn_dt_bias=gdn_dt_bias, gdn_norm=gdn_norm, gdn_w_out=gdn_w_out,
             ffn_w_in=ffn_w_in, ffn_conv_w=ffn_conv_w, ffn_conv_b=ffn_conv_b, ffn_w_out=ffn_w_out)
    n_layers, n_pool = cache_k.shape[:2]
    pool_shape = (n_layers, n_pool, D_MODEL, PAGE_SIZE)
    cache = (cache_k.transpose(0, 1, 3, 4, 2).reshape(pool_shape),
             cache_v.transpose(0, 1, 3, 4, 2).reshape(pool_shape), page_table)
    yp, p = _run_group(x_prompt, None, None, None, None, None, w)
    ys, s = _run_group(x_sample, cache, state_gla, state_gdn, state_gdn_conv, state_ffn_conv, w)
    return (yp, ys, p["k"], p["v"], s["k"], s["v"], p["gla"], s["gla"], p["gdn"], s["gdn"],
            p["gconv"], s["gconv"], p["fconv"], s["fconv"])
```

```python
import functools
import math

import jax
import jax.numpy as jnp
from jax import lax
from jax.experimental import pallas as pl
from jax.experimental.pallas import tpu as pltpu

F32 = jnp.float32
BF16 = jnp.bfloat16
HIGHEST = lax.Precision.HIGHEST

D_MODEL = 1024
DEPTH = 4
N_MIXERS = 3
NORM_EPS = 1e-6
NEG_INF = -1e30
PAGE_SIZE = 128

A_HEADS = 16
A_HEAD_DIM = D_MODEL // A_HEADS
MOBA_BLOCK = 256
MOBA_TOPK = 3

B_HEADS = 4
B_DK = D_MODEL // 2 // B_HEADS
B_DV = D_MODEL // B_HEADS
B_GATE_RANK = 16
B_GATE_TAU = 16.0
B_CHUNK = 64

C_HEADS = 8
C_DK = D_MODEL // C_HEADS
C_DV = D_MODEL // C_HEADS
C_CONV = 4
C_CHUNK = 64

D_FF = 2816
FFN_CONV = 3

LANES = 128
SUBLANES = 8
ROW_TILE = 256
FFN_COL_CHUNK = 2816
SAMPLE_PAGES_PER_STEP = 16
VMEM_LIMIT = 56 * 1024 * 1024

NT_DIMS = (((1,), (1,)), ((), ()))
TN_DIMS = (((0,), (0,)), ((), ()))


def _params(sem, vmem=VMEM_LIMIT):
    return pltpu.CompilerParams(dimension_semantics=sem, vmem_limit_bytes=vmem)


def _const_spec(shape):
    nd = len(shape)
    return pl.BlockSpec(shape, lambda *_: (0,) * nd, pipeline_mode=pl.Buffered(1))


def _rms(x, g):
    return x * lax.rsqrt(jnp.mean(x * x, axis=-1, keepdims=True) + NORM_EPS) * g


def _sigmoid(x):
    return 1.0 / (1.0 + jnp.exp(-x))


def _silu(x):
    return x * _sigmoid(x)


def _softplus(x):
    return jnp.maximum(x, 0.0) + jnp.log(1.0 + jnp.exp(-jnp.abs(x)))


def _bdot(a, b):
    return jnp.dot(a.astype(BF16), b.astype(BF16), preferred_element_type=F32)


def _bdot_nt(a, b):
    return lax.dot_general(a.astype(BF16), b.astype(BF16), NT_DIMS, preferred_element_type=F32)


def _bdot_tn(a, b):
    return lax.dot_general(a.astype(BF16), b.astype(BF16), TN_DIMS, preferred_element_type=F32)


def _cumsum_rows(x):
    n = x.shape[0]
    row = lax.broadcasted_iota(jnp.int32, x.shape, 0)
    s = 1
    while s < n:
        x = x + jnp.where(row >= s, pltpu.roll(x, s, 0), 0.0)
        s *= 2
    return x


def _shift_rows(u, tail, s):
    r = pltpu.roll(u, s, 0)
    row = lax.broadcasted_iota(jnp.int32, tail.shape, 0)
    head = jnp.where(row < s, pltpu.roll(tail, s, 0), r[0:SUBLANES])
    return jnp.concatenate([head, r[SUBLANES:]], axis=0)


def _norm_proj_kernel(x_ref, g_ref, w_ref, *refs, splits):
    out_refs = refs[len(refs) - len(splits):]
    hb = _rms(x_ref[...], g_ref[...]).astype(BF16)
    for o_ref, (c0, c1) in zip(out_refs, splits):
        o_ref[...] = jnp.dot(hb, w_ref[:, c0:c1], preferred_element_type=F32).astype(o_ref.dtype)


def _norm_proj(x2d, g, w, splits, stack=None):
    n = x2d.shape[0]
    tm = min(ROW_TILE, n)
    out_shape = [jax.ShapeDtypeStruct((n, c1 - c0), F32) for c0, c1 in splits]
    out_specs = [pl.BlockSpec((tm, c1 - c0), lambda i: (i, 0)) for c0, c1 in splits]
    in_specs = [pl.BlockSpec((tm, D_MODEL), lambda i: (i, 0)), _const_spec((1, D_MODEL)),
                _const_spec(w.shape)]
    args = [x2d, g.reshape(1, D_MODEL), w]
    aliases = {}
    if stack is not None:
        layer, n_layers, buffers = stack
        first = len(splits) - len(buffers)
        for k, buf in enumerate(buffers):
            c0, c1 = splits[first + k]
            out_shape[first + k] = jax.ShapeDtypeStruct((n_layers, n, c1 - c0), F32)
            out_specs[first + k] = pl.BlockSpec((None, tm, c1 - c0), lambda i: (layer, i, 0))
            if buf is not None:
                aliases[len(args)] = first + k
                in_specs.append(pl.BlockSpec(memory_space=pl.ANY))
                args.append(buf)
    return pl.pallas_call(
        functools.partial(_norm_proj_kernel, splits=splits),
        grid=(n // tm,),
        in_specs=in_specs,
        out_specs=out_specs,
        out_shape=out_shape,
        input_output_aliases=aliases,
        compiler_params=_params(("parallel",)),
        name="norm_proj",
    )(*args)


def _final_norm_kernel(x_ref, g_ref, o_ref):
    o_ref[...] = _rms(x_ref[...], g_ref[...])


def _final_norm(x2d, g):
    n = x2d.shape[0]
    tm = min(2 * ROW_TILE, n)
    return pl.pallas_call(
        _final_norm_kernel,
        grid=(n // tm,),
        in_specs=[pl.BlockSpec((tm, D_MODEL), lambda i: (i, 0)), _const_spec((1, D_MODEL))],
        out_specs=pl.BlockSpec((tm, D_MODEL), lambda i: (i, 0)),
        out_shape=jax.ShapeDtypeStruct((n, D_MODEL), F32),
        compiler_params=_params(("parallel",)),
        name="final_norm",
    )(x2d, g.reshape(1, D_MODEL))


def _post_kernel(*refs, per_seq_prev, tiles_per_seq):
    if per_seq_prev:
        (x_ref, o_ref, wo_ref, g_ref, win_ref, cw_ref, cb_ref, wout_ref, p1_ref, p2_ref,
         y_ref, u_ref) = refs
    else:
        (x_ref, o_ref, wo_ref, g_ref, win_ref, cw_ref, cb_ref, wout_ref,
         y_ref, tail_ref, carry_scr) = refs

        @pl.when(pl.program_id(0) % tiles_per_seq == 0)
        def _():
            carry_scr[...] = jnp.zeros_like(carry_scr)

    tm = x_ref.shape[0]
    x = x_ref[...] + jnp.dot(o_ref[...].astype(BF16), wo_ref[...], preferred_element_type=F32)
    hb = _rms(x, g_ref[...]).astype(BF16)
    if per_seq_prev:
        t_in_seq = lax.broadcasted_iota(jnp.int32, (tm, 1), 0) % SUBLANES
    acc = x
    for c in range(D_FF // FFN_COL_CHUNK):
        conv = []
        for off in (c * FFN_COL_CHUNK, D_FF + c * FFN_COL_CHUNK):
            cols = slice(off, off + FFN_COL_CHUNK)
            u = jnp.dot(hb, win_ref[:, cols], preferred_element_type=F32)
            if per_seq_prev:
                u1 = jnp.where(t_in_seq < 1, p1_ref[:, cols], pltpu.roll(u, 1, 0))
                u2 = jnp.where(t_in_seq < 2, p2_ref[:, cols], pltpu.roll(u, 2, 0))
                u_ref[:, cols] = u
            else:
                tail = carry_scr[:, cols]
                u1 = _shift_rows(u, tail, 1)
                u2 = _shift_rows(u, tail, 2)
                carry_scr[:, cols] = u[tm - SUBLANES:tm]
                tail_ref[:, cols] = u[tm - SUBLANES:tm]
            conv.append(u2 * cw_ref[0:1, cols] + u1 * cw_ref[1:2, cols] + u * cw_ref[2:3, cols]
                        + cb_ref[:, cols])
        act = (_silu(conv[0]) * conv[1]).astype(BF16)
        acc = acc + jnp.dot(act, wout_ref[c * FFN_COL_CHUNK:(c + 1) * FFN_COL_CHUNK, :],
                            preferred_element_type=F32)
    y_ref[...] = acc


def _post(x2d, o2d, w_o, g, w_in, conv_w, conv_b, w_out, seq_len, prev=None):
    n = x2d.shape[0]
    tm = min(ROW_TILE, n)
    nseq = n // seq_len
    row_spec = pl.BlockSpec((tm, D_MODEL), lambda i: (i, 0))
    in_specs = [row_spec, row_spec,
                _const_spec((D_MODEL, D_MODEL)), _const_spec((1, D_MODEL)),
                _const_spec((D_MODEL, 2 * D_FF)), _const_spec((FFN_CONV, 2 * D_FF)),
                _const_spec((1, 2 * D_FF)), _const_spec((D_FF, D_MODEL))]
    args = [x2d, o2d, w_o, g.reshape(1, D_MODEL), w_in, conv_w, conv_b.reshape(1, 2 * D_FF), w_out]
    y_shape = jax.ShapeDtypeStruct((n, D_MODEL), F32)
    if prev is None:
        assert seq_len % tm == 0
        tps = seq_len // tm
        y, tail = pl.pallas_call(
            functools.partial(_post_kernel, per_seq_prev=False, tiles_per_seq=tps),
            grid=(n // tm,),
            in_specs=in_specs,
            out_specs=[row_spec, pl.BlockSpec((None, SUBLANES, 2 * D_FF), lambda i: (i // tps, 0, 0))],
            out_shape=[y_shape, jax.ShapeDtypeStruct((nseq, SUBLANES, 2 * D_FF), F32)],
            scratch_shapes=[pltpu.VMEM((SUBLANES, 2 * D_FF), F32)],
            compiler_params=_params(("arbitrary",)),
            name="post_ffn_prompt",
        )(*args)
        return y, tail[:, SUBLANES - (FFN_CONV - 1):]
    assert seq_len == SUBLANES and n == tm
    pad = ((0, 0), (0, SUBLANES - 1), (0, 0))
    p1 = jnp.pad(prev[:, 1:2], pad).reshape(n, 2 * D_FF)
    p2 = jnp.pad(prev, ((0, 0), (0, SUBLANES - 2), (0, 0))).reshape(n, 2 * D_FF)
    wide_spec = pl.BlockSpec((tm, 2 * D_FF), lambda i: (i, 0))
    y, u = pl.pallas_call(
        functools.partial(_post_kernel, per_seq_prev=True, tiles_per_seq=1),
        grid=(1,),
        in_specs=in_specs + [wide_spec, wide_spec],
        out_specs=[row_spec, wide_spec],
        out_shape=[y_shape, jax.ShapeDtypeStruct((n, 2 * D_FF), F32)],
        compiler_params=_params(("arbitrary",)),
        name="post_ffn_sample",
    )(*args, p1, p2)
    return y, u.reshape(nseq, seq_len, 2 * D_FF)[:, seq_len - (FFN_CONV - 1):]


def _topk_bias_t(gate_t, n_valid, n_rows):
    row = lax.broadcasted_iota(jnp.int32, gate_t.shape, 0)
    valid = row < n_valid
    out = []
    for n in range(n_rows):
        gn = gate_t[n:n + 1, :]
        beats = jnp.where(gate_t > gn, 1.0, jnp.where((gate_t == gn) & (row < n), 1.0, 0.0))
        rank = jnp.sum(jnp.where(valid, beats, 0.0), axis=0, keepdims=True)
        out.append(jnp.where(rank < MOBA_TOPK, 0.0, NEG_INF))
    return out


def _moba_prompt_tile(c, q_ref, o_ref, kb_scr, vt_scr, km_scr):
    blk = MOBA_BLOCK
    hd = A_HEAD_DIM
    q = q_ref[...]
    lane = lax.broadcasted_iota(jnp.int32, (1, LANES), 1)
    causal = (lax.broadcasted_iota(jnp.int32, (blk, blk), 0)
              <= lax.broadcasted_iota(jnp.int32, (blk, blk), 1))
    pair = range(2)
    qh = [jnp.where(lane // hd == hh, q, 0.0) for hh in pair]
    qs = [(x * (hd ** -0.5)).astype(BF16) for x in qh]
    s = [[lax.dot_general(kb_scr[n], qs[hh], NT_DIMS, preferred_element_type=F32) for n in range(c + 1)]
         for hh in pair]
    for hh in pair:
        s[hh][c] = jnp.where(causal, s[hh][c], NEG_INF)
        if c > MOBA_TOPK:
            gate_t = lax.dot_general(km_scr[...], qh[hh], NT_DIMS, precision=HIGHEST,
                                     preferred_element_type=F32)
            for n, b in enumerate(_topk_bias_t(gate_t, c, c)):
                s[hh][n] = s[hh][n] + b
    m = []
    for hh in pair:
        mv = s[hh][c]
        for n in range(c):
            mv = jnp.maximum(mv, s[hh][n])
        m.append(jnp.max(mv, axis=0, keepdims=True))
    o_t = []
    for hh in pair:
        lv = None
        acc = None
        for n in range(c + 1):
            p = jnp.exp(s[hh][n] - m[hh])
            lv = p if lv is None else lv + p
            pv = jnp.dot(vt_scr[n, hh * hd:(hh + 1) * hd, :], p.astype(BF16),
                         preferred_element_type=F32)
            acc = pv if acc is None else acc + pv
        o_t.append(acc / jnp.sum(lv, axis=0, keepdims=True))
    o_ref[...] = jnp.concatenate(o_t, axis=0).T.astype(o_ref.dtype)


def _moba_prompt_kernel(q_ref, k_ref, v_ref, o_ref, kb_scr, vt_scr, km_scr, *, nb):
    i = pl.program_id(2)
    blk = MOBA_BLOCK

    @pl.when(i == 0)
    def _():
        km_scr[...] = jnp.zeros_like(km_scr)
        for n in range(nb):
            kn = k_ref[n * blk:(n + 1) * blk, :]
            kb_scr[n] = kn.astype(BF16)
            vt_scr[n] = v_ref[n * blk:(n + 1) * blk, :].T.astype(BF16)
            km_scr[n:n + 1, :] = jnp.mean(kn, axis=0, keepdims=True)

    for c in range(nb):
        pl.when(i == c)(functools.partial(_moba_prompt_tile, c, q_ref, o_ref, kb_scr, vt_scr, km_scr))


def _moba_prompt_attn(q, k, v, layer):
    bsz, s, _ = q.shape
    assert s % MOBA_BLOCK == 0
    nb = s // MOBA_BLOCK
    nb_pad = -(-nb // SUBLANES) * SUBLANES
    hp = D_MODEL // LANES
    q_spec = pl.BlockSpec((None, MOBA_BLOCK, LANES), lambda b, h, i: (b, i, h))
    kv_spec = pl.BlockSpec((None, None, s, LANES), lambda b, h, i: (layer, b, 0, h))
    return pl.pallas_call(
        functools.partial(_moba_prompt_kernel, nb=nb),
        grid=(bsz, hp, nb),
        in_specs=[q_spec, kv_spec, kv_spec],
        out_specs=q_spec,
        out_shape=jax.ShapeDtypeStruct((bsz, s, D_MODEL), BF16),
        scratch_shapes=[pltpu.VMEM((nb, MOBA_BLOCK, LANES), BF16),
                        pltpu.VMEM((nb, LANES, MOBA_BLOCK), BF16),
                        pltpu.VMEM((nb_pad, LANES), F32)],
        compiler_params=_params(("parallel", "parallel", "arbitrary")),
        name="moba_prompt_attn",
    )(q, k, v)


def _topk_select(gate, n_valid, n_cols):
    col = lax.broadcasted_iota(jnp.int32, gate.shape, 1)
    valid = col < n_valid
    sel = []
    for n in range(n_cols):
        gn = gate[:, n:n + 1]
        beats = jnp.where(gate > gn, 1.0, jnp.where((gate == gn) & (col < n), 1.0, 0.0))
        rank = jnp.sum(jnp.where(valid, beats, 0.0), axis=1, keepdims=True)
        sel.append(rank < MOBA_TOPK)
    return sel


def _moba_sample_kernel(pt_ref, q_ref, kn_ref, vn_ref, *refs, n_pages, t_new):
    g_pages = SAMPLE_PAGES_PER_STEP
    k_refs = refs[:g_pages]
    v_refs = refs[g_pages:2 * g_pages]
    o_ref = refs[2 * g_pages]
    qbf_scr, new_scr, gate_scr, sc_scr, l_scr, acc_scr = refs[2 * g_pages + 1:]
    del pt_ref
    j = pl.program_id(1)
    n_steps = n_pages // g_pages
    n_blocks = n_pages * PAGE_SIZE // MOBA_BLOCK
    pages_per_block = MOBA_BLOCK // PAGE_SIZE
    blocks_per_step = g_pages // pages_per_block
    rows = A_HEADS * t_new
    head_of_row = lax.broadcasted_iota(jnp.int32, (rows, D_MODEL), 0) // t_new
    head_of_lane = lax.broadcasted_iota(jnp.int32, (rows, D_MODEL), 1) // A_HEAD_DIM

    @pl.when(j == 0)
    def _():
        qbd = jnp.where(head_of_row == head_of_lane,
                        jnp.concatenate([q_ref[...]] * A_HEADS, axis=0), 0.0)
        qbf_scr[...] = (qbd * (A_HEAD_DIM ** -0.5)).astype(BF16)
        gate_scr[...] = jnp.zeros_like(gate_scr)

    @pl.when(j < n_steps)
    def _():
        col = lax.broadcasted_iota(jnp.int32, gate_scr.shape, 1)
        gate = gate_scr[...]
        for bb in range(blocks_per_step):
            ssum = None
            for g in range(bb * pages_per_block, (bb + 1) * pages_per_block):
                s = jnp.dot(qbf_scr[...], k_refs[g][...].astype(BF16), preferred_element_type=F32)
                sc_scr[j * g_pages + g] = s
                ssum = s if ssum is None else ssum + s
            gmean = jnp.sum(ssum, axis=1, keepdims=True) * (1.0 / MOBA_BLOCK)
            gate = jnp.where(col == j * blocks_per_step + bb, gmean, gate)
        gate_scr[...] = gate

    @pl.when(j == n_steps - 1)
    def _():
        sel = _topk_select(gate_scr[...], n_blocks, n_blocks)
        new_scr[...] = jnp.zeros_like(new_scr)
        new_scr[0, 0:t_new, :] = kn_ref[...].astype(BF16)
        new_scr[1, 0:t_new, :] = vn_ref[...].astype(BF16)
        s_own = lax.dot_general(qbf_scr[...], new_scr[0], NT_DIMS, preferred_element_type=F32)
        r_tok = lax.broadcasted_iota(jnp.int32, s_own.shape, 0) % t_new
        c_tok = lax.broadcasted_iota(jnp.int32, s_own.shape, 1)
        s_own = jnp.where(c_tok <= r_tok, s_own, NEG_INF)
        mv = s_own
        for b in range(n_blocks):
            bias = jnp.broadcast_to(jnp.where(sel[b], 0.0, NEG_INF), s_own.shape)
            for p in range(b * pages_per_block, (b + 1) * pages_per_block):
                mv = jnp.maximum(mv, sc_scr[p] + bias)
        m = jnp.max(mv, axis=1, keepdims=True)
        p_own = jnp.exp(s_own - m)
        lv = p_own
        for b in range(n_blocks):
            shift = jnp.broadcast_to(jnp.where(sel[b], 0.0, NEG_INF) - m, s_own.shape)
            for p in range(b * pages_per_block, (b + 1) * pages_per_block):
                e = jnp.exp(sc_scr[p] + shift)
                lv = lv + e
                sc_scr[p] = e
        l_scr[...] = jnp.sum(lv, axis=1, keepdims=True)
        acc_scr[...] = jnp.dot(p_own.astype(BF16), new_scr[1], preferred_element_type=F32)

    @pl.when(j >= n_steps)
    def _():
        acc = acc_scr[...]
        for g in range(g_pages):
            pp = sc_scr[(j - n_steps) * g_pages + g].astype(BF16)
            acc = acc + lax.dot_general(pp, v_refs[g][...].astype(BF16), NT_DIMS,
                                        preferred_element_type=F32)
        acc_scr[...] = acc

    @pl.when(j == 2 * n_steps - 1)
    def _():
        acc = jnp.where(head_of_row == head_of_lane, acc_scr[...] / l_scr[...], 0.0)
        o = acc[0:t_new]
        for h in range(1, A_HEADS):
            o = o + acc[h * t_new:(h + 1) * t_new]
        o_ref[...] = o


def _moba_sample_attn(q, k_new, v_new, cache_kt, cache_vt, page_table, layer):
    bsz, t_new, _ = q.shape
    n_pages = page_table.shape[1]
    g_pages = SAMPLE_PAGES_PER_STEP
    assert (n_pages * PAGE_SIZE) % MOBA_BLOCK == 0 and MOBA_BLOCK % PAGE_SIZE == 0
    assert (g_pages * PAGE_SIZE) % MOBA_BLOCK == 0 and n_pages % g_pages == 0
    assert t_new == SUBLANES and A_HEADS * t_new == LANES and PAGE_SIZE == LANES
    n_steps = n_pages // g_pages
    n_blocks = n_pages * PAGE_SIZE // MOBA_BLOCK
    assert MOBA_TOPK <= n_blocks <= LANES
    tok_spec = pl.BlockSpec((None, t_new, D_MODEL), lambda b, j, pt: (b, 0, 0))
    new_spec = pl.BlockSpec((None, None, t_new, D_MODEL), lambda b, j, pt: (layer, b, 0, 0))

    def k_spec(g):
        return pl.BlockSpec(
            (None, None, D_MODEL, PAGE_SIZE),
            lambda b, j, pt: (layer, pt[b, g_pages * jnp.minimum(j, n_steps - 1) + g], 0, 0))

    def v_spec(g):
        def index(b, j, pt):
            in_v = j >= n_steps
            seq = jnp.where(in_v, b, jnp.maximum(b - 1, 0))
            step = jnp.where(in_v, j - n_steps, n_steps - 1)
            return (layer, pt[seq, g_pages * step + g], 0, 0)
        return pl.BlockSpec((None, None, D_MODEL, PAGE_SIZE), index)

    rows = A_HEADS * t_new
    grid_spec = pltpu.PrefetchScalarGridSpec(
        num_scalar_prefetch=1,
        grid=(bsz, 2 * n_steps),
        in_specs=[tok_spec, new_spec, new_spec] + [k_spec(g) for g in range(g_pages)]
        + [v_spec(g) for g in range(g_pages)],
        out_specs=tok_spec,
        scratch_shapes=[pltpu.VMEM((rows, D_MODEL), BF16),
                        pltpu.VMEM((2, LANES, D_MODEL), BF16), pltpu.VMEM((rows, LANES), F32),
                        pltpu.VMEM((n_pages, rows, PAGE_SIZE), F32), pltpu.VMEM((rows, 1), F32),
                        pltpu.VMEM((rows, D_MODEL), F32)],
    )
    return pl.pallas_call(
        functools.partial(_moba_sample_kernel, n_pages=n_pages, t_new=t_new),
        grid_spec=grid_spec,
        out_shape=jax.ShapeDtypeStruct((bsz, t_new, D_MODEL), F32),
        compiler_params=_params(("parallel", "arbitrary")),
        name="moba_sample_attn",
    )(page_table, q, k_new, v_new, *([cache_kt] * g_pages), *([cache_vt] * g_pages))


def _gla_kernel(q_ref, k_ref, v_ref, r_ref, glr_ref, wg_ref, bg_ref, ng_ref, s0_ref,
                o_ref, sout_ref, s_scr, *, valid_rows):
    ci = pl.program_id(1)
    c = q_ref.shape[0]

    @pl.when(ci == 0)
    def _():
        s_scr[...] = s0_ref[...]

    x = jnp.dot(glr_ref[...].astype(BF16), wg_ref[...], preferred_element_type=F32) + bg_ref[...]
    gk = -_softplus(-x) * (1.0 / B_GATE_TAU)
    if valid_rows < c:
        gk = jnp.where(lax.broadcasted_iota(jnp.int32, gk.shape, 0) < valid_rows, gk, 0.0)
    b = _cumsum_rows(gk)
    b_last = b[c - 1:c, :]
    k = k_ref[...]
    q_t = (q_ref[...] * (B_DK ** -0.5) * jnp.exp(b)).astype(BF16)
    k_t = (k * jnp.exp(-b)).astype(BF16)
    k_end = (k * jnp.exp(b_last - b)).astype(BF16)
    e_last = jnp.exp(b_last)
    causal = (lax.broadcasted_iota(jnp.int32, (c, c), 1) <= lax.broadcasted_iota(jnp.int32, (c, c), 0))
    for h in range(B_HEADS):
        kc = slice(h * B_DK, (h + 1) * B_DK)
        vc = slice(h * B_DV, (h + 1) * B_DV)
        vh = v_ref[:, vc].astype(BF16)
        st = s_scr[h]
        att = jnp.where(causal, lax.dot_general(q_t[:, kc], k_t[:, kc], NT_DIMS,
                                                preferred_element_type=F32), 0.0)
        o = (jnp.dot(att.astype(BF16), vh, preferred_element_type=F32)
             + lax.dot_general(q_t[:, kc], st.astype(BF16), NT_DIMS, preferred_element_type=F32))
        s_scr[h] = st * e_last[:, kc] + lax.dot_general(vh, k_end[:, kc], TN_DIMS,
                                                        preferred_element_type=F32)
        o = o * lax.rsqrt(jnp.mean(o * o, axis=-1, keepdims=True) + NORM_EPS) * ng_ref[...]
        o_ref[:, vc] = (o * _silu(r_ref[:, vc])).astype(o_ref.dtype)

    @pl.when(ci == pl.num_programs(1) - 1)
    def _():
        sout_ref[...] = s_scr[...]


def _gla_scan(q, k, v, r, glr, w_gate, b_gate, norm_g, s0_t, valid_rows):
    bsz, t, _ = q.shape
    c = B_CHUNK
    assert t % c == 0
    nk = B_HEADS * B_DK

    def tok(width):
        return pl.BlockSpec((None, c, width), lambda b, ci: (b, ci, 0))

    st_spec = pl.BlockSpec((None, B_HEADS, B_DV, B_DK), lambda b, ci: (b, 0, 0, 0))
    return pl.pallas_call(
        functools.partial(_gla_kernel, valid_rows=valid_rows),
        grid=(bsz, t // c),
        in_specs=[tok(nk), tok(nk), tok(D_MODEL), tok(D_MODEL), tok(LANES),
                  _const_spec((LANES, nk)), _const_spec((1, nk)), _const_spec((1, B_DV)), st_spec],
        out_specs=[tok(D_MODEL), st_spec],
        out_shape=[jax.ShapeDtypeStruct((bsz, t, D_MODEL), BF16),
                   jax.ShapeDtypeStruct((bsz, B_HEADS, B_DV, B_DK), F32)],
        scratch_shapes=[pltpu.VMEM((B_HEADS, B_DV, B_DK), F32)],
        compiler_params=_params(("parallel", "arbitrary")),
        name="gla_scan",
    )(q, k, v, r, glr, w_gate, b_gate.reshape(1, nk), norm_g.reshape(1, B_DV), s0_t)


def _split_bf16(x):
    hi = x.astype(BF16)
    return hi, (x - hi.astype(F32)).astype(BF16)


def _dot3(a, b, dims=(((1,), (0,)), ((), ()))):
    ah, al = _split_bf16(a)
    bh, bl = _split_bf16(b)

    def f(x, y):
        return lax.dot_general(x, y, dims, preferred_element_type=F32)
    return f(ah, bh) + (f(ah, bl) + f(al, bh))


def _unit_lower_inverse_minus_eye(a_list):
    c = a_list[0].shape[0]
    x = [-a for a in a_list]
    n = list(x)
    p = 2
    while p < c:
        x = [_dot3(xi, xi) for xi in x]
        n = [ni + xi + _dot3(ni, xi) for ni, xi in zip(n, x)]
        p *= 2
    return n


def _gdn_kernel(qkv_ref, z_ref, ba_ref, cw_ref, alog_ref, dtb_ref, ng_ref, s0_ref,
                o_ref, sout_ref, s_scr, carry_scr, *, valid_lo, valid_hi):
    ci = pl.program_id(1)
    c = qkv_ref.shape[0]
    heads = range(C_HEADS)

    @pl.when(ci == 0)
    def _():
        s_scr[...] = s0_ref[...]
        carry_scr[0:SUBLANES, :] = jnp.zeros((SUBLANES, carry_scr.shape[1]), F32)

    u = qkv_ref[...]
    carry_scr[SUBLANES:SUBLANES + c, :] = u
    conv = u * cw_ref[C_CONV - 1:C_CONV, :]
    for s in range(1, C_CONV):
        conv = conv + carry_scr[SUBLANES - s:SUBLANES - s + c, :] * cw_ref[C_CONV - 1 - s:C_CONV - s, :]
    carry_scr[0:SUBLANES, :] = u[c - SUBLANES:c]
    xc = _silu(conv)

    ba = ba_ref[...]
    beta_all = _sigmoid(ba)
    g_all = -jnp.exp(alog_ref[...]) * _softplus(ba + dtb_ref[...])
    if valid_lo > 0 or valid_hi < c:
        rowi = lax.broadcasted_iota(jnp.int32, ba.shape, 0)
        ok = (rowi >= valid_lo) & (rowi < valid_hi)
        beta_all = jnp.where(ok, beta_all, 0.0)
        g_all = jnp.where(ok, g_all, 0.0)
    gam_all = _cumsum_rows(g_all)
    gam_t = jnp.concatenate([gam_all, jnp.zeros((LANES - c, LANES), F32)], axis=0).T
    ri = lax.broadcasted_iota(jnp.int32, (c, c), 0)
    cj = lax.broadcasted_iota(jnp.int32, (c, c), 1)
    incl = cj <= ri
    strict = cj < ri

    qn, kn, gcol, dec, rhs, a_mat = [], [], [], [], [], []
    for h in heads:
        qh = xc[:, h * C_DK:(h + 1) * C_DK]
        kh = xc[:, D_MODEL + h * C_DK:D_MODEL + (h + 1) * C_DK]
        vh = xc[:, 2 * D_MODEL + h * C_DV:2 * D_MODEL + (h + 1) * C_DV]
        qn.append(qh * lax.rsqrt(jnp.sum(qh * qh, axis=-1, keepdims=True) + NORM_EPS) * (C_DK ** -0.5))
        kn.append(kh * lax.rsqrt(jnp.sum(kh * kh, axis=-1, keepdims=True) + NORM_EPS))
        beta = beta_all[:, h:h + 1]
        gcol.append(gam_all[:, C_HEADS + h:C_HEADS + h + 1])
        grow = gam_t[C_HEADS + h:C_HEADS + h + 1, 0:c]
        dec.append(jnp.where(incl, jnp.exp(jnp.where(incl, gcol[h] - grow, 0.0)), 0.0))
        kb = kn[h] * beta
        rhs.append(jnp.concatenate([kb * jnp.exp(gcol[h]), vh * beta], axis=1))
        a_mat.append(jnp.where(strict, _dot3(kb, kn[h], NT_DIMS) * dec[h], 0.0))
    inv_m = _unit_lower_inverse_minus_eye(a_mat)
    wu = [rhs[h] + _dot3(inv_m[h], rhs[h]) for h in heads]
    att = [_bdot_nt(qn[h], kn[h]) * dec[h] for h in heads]

    st = [s_scr[h] for h in heads]
    stb = [s.astype(BF16) for s in st]
    v_new = [wu[h][:, C_DK:] - lax.dot_general(wu[h][:, :C_DK].astype(BF16), stb[h], NT_DIMS,
                                               preferred_element_type=F32) for h in heads]
    o = [lax.dot_general((qn[h] * jnp.exp(gcol[h])).astype(BF16), stb[h], NT_DIMS,
                         preferred_element_type=F32) + _bdot(att[h], v_new[h]) for h in heads]
    for h in heads:
        g_last = gcol[h][c - 1:c, :]
        s_scr[h] = jnp.exp(g_last) * st[h] + _bdot_tn(v_new[h], kn[h] * jnp.exp(g_last - gcol[h]))
    for h in heads:
        hc = slice(h * C_DV, (h + 1) * C_DV)
        oh = o[h] * lax.rsqrt(jnp.mean(o[h] * o[h], axis=-1, keepdims=True) + NORM_EPS) * ng_ref[...]
        o_ref[:, hc] = (oh * _silu(z_ref[:, hc])).astype(o_ref.dtype)

    @pl.when(ci == pl.num_programs(1) - 1)
    def _():
        sout_ref[...] = s_scr[...]


def _gdn_scan(qkv, z, ba, conv_w, a_log, dt_bias, norm_g, s0_t, valid_lo, valid_hi):
    bsz, t, _ = qkv.shape
    c = C_CHUNK
    assert t % c == 0

    def tok(width):
        return pl.BlockSpec((None, c, width), lambda b, ci: (b, ci, 0))

    def under_a(p):
        return jnp.zeros((1, LANES), F32).at[0, C_HEADS:2 * C_HEADS].set(p)

    st_spec = pl.BlockSpec((None, C_HEADS, C_DV, C_DK), lambda b, ci: (b, 0, 0, 0))
    return pl.pallas_call(
        functools.partial(_gdn_kernel, valid_lo=valid_lo, valid_hi=valid_hi),
        grid=(bsz, t // c),
        in_specs=[tok(3 * D_MODEL), tok(D_MODEL), tok(LANES),
                  _const_spec((C_CONV, 3 * D_MODEL)), _const_spec((1, LANES)),
                  _const_spec((1, LANES)), _const_spec((1, C_DV)), st_spec],
        out_specs=[tok(D_MODEL), st_spec],
        out_shape=[jax.ShapeDtypeStruct((bsz, t, D_MODEL), BF16),
                   jax.ShapeDtypeStruct((bsz, C_HEADS, C_DV, C_DK), F32)],
        scratch_shapes=[pltpu.VMEM((C_HEADS, C_DV, C_DK), F32),
                        pltpu.VMEM((SUBLANES + c, 3 * D_MODEL), F32)],
        compiler_params=_params(("parallel", "arbitrary")),
        name="gdn_scan",
    )(qkv, z, ba, conv_w, under_a(a_log), under_a(dt_bias), norm_g.reshape(1, C_DV), s0_t)


def _pad_cols(w, width):
    return jnp.pad(w, ((0, 0), (0, width - w.shape[1])))


def _pad_tokens(a, before, total):
    return jnp.pad(a, ((0, 0), (before, total - before - a.shape[1]), (0, 0)))


def _moba_mixer(x, g, w_in, layer, n_layers, kv_rows, cache=None):
    bsz, t, _ = x.shape
    splits = tuple((i * D_MODEL, (i + 1) * D_MODEL) for i in range(3))
    q, k, v = _norm_proj(x.reshape(bsz * t, D_MODEL), g, w_in.astype(BF16), splits,
                         stack=(layer, n_layers, kv_rows))
    q = q.reshape(bsz, t, D_MODEL)
    k4 = k.reshape(n_layers, bsz, t, D_MODEL)
    v4 = v.reshape(n_layers, bsz, t, D_MODEL)
    if cache is None:
        o = _moba_prompt_attn(q, k4, v4, layer)
    else:
        o = _moba_sample_attn(q, k4, v4, *cache, layer)
    return o.reshape(bsz * t, D_MODEL), (k, v)


def _gla_mixer(x, g, w_in, w_gate, b_gate, norm_g, s0):
    bsz, t, _ = x.shape
    nk = B_HEADS * B_DK
    edges = (0, nk, 2 * nk, 2 * nk + D_MODEL, 2 * nk + 2 * D_MODEL, 2 * nk + 2 * D_MODEL + LANES)
    splits = tuple(zip(edges[:-1], edges[1:]))
    parts = _norm_proj(x.reshape(bsz * t, D_MODEL), g, _pad_cols(w_in, edges[-1]).astype(BF16), splits)
    t_pad = -(-t // B_CHUNK) * B_CHUNK
    q, k, v, r, glr = (_pad_tokens(a.reshape(bsz, t, -1), 0, t_pad) for a in parts)
    wg = jnp.pad(w_gate, ((0, LANES - B_GATE_RANK), (0, 0))).astype(BF16)
    if s0 is None:
        s0_t = jnp.zeros((bsz, B_HEADS, B_DV, B_DK), F32)
    else:
        s0_t = jnp.swapaxes(s0, -1, -2)
    o, s_t = _gla_scan(q, k, v, r, glr, wg, b_gate, norm_g, s0_t, min(t, B_CHUNK))
    return o[:, :t].reshape(bsz * t, D_MODEL), jnp.swapaxes(s_t, -1, -2)


def _gdn_mixer(x, g, w_in, conv_w, a_log, dt_bias, norm_g, s0, conv_prev):
    bsz, t, _ = x.shape
    edges = (0, 3 * D_MODEL, 4 * D_MODEL, 4 * D_MODEL + LANES)
    splits = tuple(zip(edges[:-1], edges[1:]))
    parts = _norm_proj(x.reshape(bsz * t, D_MODEL), g, _pad_cols(w_in, edges[-1]).astype(BF16), splits)
    qkv, z, ba = (a.reshape(bsz, t, -1) for a in parts)
    hist = C_CONV - 1
    if conv_prev is None:
        lo = 0
        conv_new = qkv[:, t - hist:]
        s0_t = jnp.zeros((bsz, C_HEADS, C_DV, C_DK), F32)
        qkv_in = qkv
    else:
        lo = hist
        assert t >= hist
        conv_new = qkv[:, t - hist:]
        s0_t = jnp.swapaxes(s0, -1, -2)
        qkv_in = jnp.concatenate([conv_prev, qkv], axis=1)
    t_pad = -(-(lo + t) // C_CHUNK) * C_CHUNK
    if t_pad != lo + t or lo:
        assert t_pad == C_CHUNK
    qkv_in = _pad_tokens(qkv_in, 0, t_pad)
    z = _pad_tokens(z, lo, t_pad)
    ba = _pad_tokens(ba, lo, t_pad)
    o, s_t = _gdn_scan(qkv_in, z, ba, conv_w, a_log, dt_bias, norm_g, s0_t, lo, min(lo + t, C_CHUNK))
    return o[:, lo:lo + t].reshape(bsz * t, D_MODEL), jnp.swapaxes(s_t, -1, -2), conv_new


def _run_group(x, cache, state_gla, state_gdn, state_gdn_conv, state_ffn_conv, w):
    bsz, t, _ = x.shape
    x2d = x.reshape(bsz * t, D_MODEL)
    out = {name: [] for name in ("gla", "gdn", "gconv", "fconv")}
    n_moba = w["moba_w_in"].shape[0]
    kv_rows = (None, None)
    for i in range(DEPTH):
        j = i // N_MIXERS
        xin = x2d.reshape(bsz, t, D_MODEL)
        if i % N_MIXERS == 0:
            o, kv_rows = _moba_mixer(xin, w["norm_mix"][i], w["moba_w_in"][j], j, n_moba, kv_rows, cache)
            w_o = w["moba_w_out"][j]
        elif i % N_MIXERS == 1:
            s0 = None if state_gla is None else state_gla[j]
            o, s = _gla_mixer(xin, w["norm_mix"][i], w["gla_w_in"][j], w["gla_w_gate"][j],
                              w["gla_b_gate"][j], w["gla_norm"][j], s0)
            out["gla"].append(s)
            w_o = w["gla_w_out"][j]
        else:
            s0 = None if state_gdn is None else state_gdn[j]
            cp = None if state_gdn_conv is None else state_gdn_conv[j]
            o, s, cn = _gdn_mixer(xin, w["norm_mix"][i], w["gdn_w_in"][j], w["gdn_conv_w"][j],
                                  w["gdn_a_log"][j], w["gdn_dt_bias"][j], w["gdn_norm"][j], s0, cp)
            out["gdn"].append(s)
            out["gconv"].append(cn)
            w_o = w["gdn_w_out"][j]
        prev = None if state_ffn_conv is None else state_ffn_conv[i]
        x2d, fc = _post(x2d, o, w_o.astype(BF16), w["norm_ffn"][i], w["ffn_w_in"][i].astype(BF16),
                        w["ffn_conv_w"][i], w["ffn_conv_b"][i], w["ffn_w_out"][i].astype(BF16), t, prev)
        out["fconv"].append(fc)
    y = _final_norm(x2d, w["norm_final"]).reshape(bsz, t, D_MODEL)
    res = {name: jnp.stack(v) for name, v in out.items()}
    rows = (n_moba, bsz, t, A_HEADS, A_HEAD_DIM)
    res["k"], res["v"] = kv_rows[0].reshape(rows), kv_rows[1].reshape(rows)
    return y, res


def kernel(x_prompt, x_sample, cache_k, cache_v, page_table, state_gla, state_gdn, state_gdn_conv,
           state_ffn_conv, norm_mix, norm_ffn, norm_final, moba_w_in, moba_w_out, gla_w_in, gla_w_gate,
           gla_b_gate, gla_norm, gla_w_out, gdn_w_in, gdn_conv_w, gdn_a_log, gdn_dt_bias, gdn_norm,
           gdn_w_out, ffn_w_in, ffn_conv_w, ffn_conv_b, ffn_w_out):
    w = dict(norm_mix=norm_mix, norm_ffn=norm_ffn, norm_final=norm_final, moba_w_in=moba_w_in,
             moba_w_out=moba_w_out, gla_w_in=gla_w_in, gla_w_gate=gla_w_gate, gla_b_gate=gla_b_gate,
             gla_norm=gla_norm, gla_w_out=gla_w_out, gdn_w_in=gdn_w_in, gdn_conv_w=gdn_conv_w,
             gdn_a_log=gdn_a_log, gdn_dt_bias=gdn_dt_bias, gdn_norm=gdn_norm, gdn_w_out=gdn_w_out,
             ffn_w_in=ffn_w_in, ffn_conv_w=ffn_conv_w, ffn_conv_b=ffn_conv_b, ffn_w_out=ffn_w_out)
    n_layers, n_pool = cache_k.shape[:2]
    pool_shape = (n_layers, n_pool, D_MODEL, PAGE_SIZE)
    cache = (cache_k.transpose(0, 1, 3, 4, 2).reshape(pool_shape),
             cache_v.transpose(0, 1, 3, 4, 2).reshape(pool_shape), page_table)
    yp, p = _run_group(x_prompt, None, None, None, None, None, w)
    ys, s = _run_group(x_sample, cache, state_gla, state_gdn, state_gdn_conv, state_ffn_conv, w)
    return (yp, ys, p["k"], p["v"], s["k"], s["v"], p["gla"], s["gla"], p["gdn"], s["gdn"],
            p["gconv"], s["gconv"], p["fconv"], s["fconv"])
```

```python
import functools
import math

import jax
import jax.numpy as jnp
from jax import lax
from jax.experimental import pallas as pl
from jax.experimental.pallas import tpu as pltpu

F32 = jnp.float32
BF16 = jnp.bfloat16
HIGHEST = lax.Precision.HIGHEST

D_MODEL = 1024
DEPTH = 4
N_MIXERS = 3
NORM_EPS = 1e-6
NEG_INF = -1e30
PAGE_SIZE = 128

A_HEADS = 16
A_HEAD_DIM = D_MODEL // A_HEADS
MOBA_BLOCK = 256
MOBA_TOPK = 3

B_HEADS = 4
B_DK = D_MODEL // 2 // B_HEADS
B_DV = D_MODEL // B_HEADS
B_GATE_RANK = 16
B_GATE_TAU = 16.0
B_CHUNK = 64

C_HEADS = 8
C_DK = D_MODEL // C_HEADS
C_DV = D_MODEL // C_HEADS
C_CONV = 4
C_CHUNK = 64

D_FF = 2816
FFN_CONV = 3

LANES = 128
SUBLANES = 8
ROW_TILE = 512
FFN_COL_CHUNK = 2816
SAMPLE_PAGES_PER_STEP = 16
VMEM_LIMIT = 56 * 1024 * 1024

NT_DIMS = (((1,), (1,)), ((), ()))
TN_DIMS = (((0,), (0,)), ((), ()))


def _params(sem, vmem=VMEM_LIMIT):
    return pltpu.CompilerParams(dimension_semantics=sem, vmem_limit_bytes=vmem)


def _const_spec(shape):
    nd = len(shape)
    return pl.BlockSpec(shape, lambda *_: (0,) * nd, pipeline_mode=pl.Buffered(1))


def _rms(x, g):
    return x * lax.rsqrt(jnp.mean(x * x, axis=-1, keepdims=True) + NORM_EPS) * g


def _sigmoid(x):
    return 1.0 / (1.0 + jnp.exp(-x))


def _silu(x):
    return x * _sigmoid(x)


def _softplus(x):
    return jnp.maximum(x, 0.0) + jnp.log(1.0 + jnp.exp(-jnp.abs(x)))


def _bdot(a, b):
    return jnp.dot(a.astype(BF16), b.astype(BF16), preferred_element_type=F32)


def _bdot_nt(a, b):
    return lax.dot_general(a.astype(BF16), b.astype(BF16), NT_DIMS, preferred_element_type=F32)


def _bdot_tn(a, b):
    return lax.dot_general(a.astype(BF16), b.astype(BF16), TN_DIMS, preferred_element_type=F32)


def _cumsum_rows(x):
    n = x.shape[0]
    row = lax.broadcasted_iota(jnp.int32, x.shape, 0)
    s = 1
    while s < n:
        x = x + jnp.where(row >= s, pltpu.roll(x, s, 0), 0.0)
        s *= 2
    return x


def _shift_rows(u, tail, s):
    r = pltpu.roll(u, s, 0)
    row = lax.broadcasted_iota(jnp.int32, tail.shape, 0)
    head = jnp.where(row < s, pltpu.roll(tail, s, 0), r[0:SUBLANES])
    return jnp.concatenate([head, r[SUBLANES:]], axis=0)


def _norm_proj_kernel(x_ref, g_ref, w_ref, *refs, splits):
    out_refs = refs[len(refs) - len(splits):]
    hb = _rms(x_ref[...], g_ref[...]).astype(BF16)
    for o_ref, (c0, c1) in zip(out_refs, splits):
        o_ref[...] = jnp.dot(hb, w_ref[:, c0:c1], preferred_element_type=F32).astype(o_ref.dtype)


def _norm_proj(x2d, g, w, splits, stack=None):
    n = x2d.shape[0]
    tm = min(ROW_TILE, n)
    out_shape = [jax.ShapeDtypeStruct((n, c1 - c0), F32) for c0, c1 in splits]
    out_specs = [pl.BlockSpec((tm, c1 - c0), lambda i: (i, 0)) for c0, c1 in splits]
    in_specs = [pl.BlockSpec((tm, D_MODEL), lambda i: (i, 0)), _const_spec((1, D_MODEL)),
                _const_spec(w.shape)]
    args = [x2d, g.reshape(1, D_MODEL), w]
    aliases = {}
    if stack is not None:
        layer, n_layers, buffers = stack
        first = len(splits) - len(buffers)
        for k, buf in enumerate(buffers):
            c0, c1 = splits[first + k]
            out_shape[first + k] = jax.ShapeDtypeStruct((n_layers, n, c1 - c0), F32)
            out_specs[first + k] = pl.BlockSpec((None, tm, c1 - c0), lambda i: (layer, i, 0))
            if buf is not None:
                aliases[len(args)] = first + k
                in_specs.append(pl.BlockSpec(memory_space=pl.ANY))
                args.append(buf)
    return pl.pallas_call(
        functools.partial(_norm_proj_kernel, splits=splits),
        grid=(n // tm,),
        in_specs=in_specs,
        out_specs=out_specs,
        out_shape=out_shape,
        input_output_aliases=aliases,
        compiler_params=_params(("parallel",)),
        name="norm_proj",
    )(*args)


def _norm_qkv_t_kernel(x_ref, g_ref, w_ref, *refs):
    q_ref, k_ref, kt_ref, vt_ref, v_scr = refs[len(refs) - 5:]
    d = D_MODEL
    hb = _rms(x_ref[...], g_ref[...]).astype(BF16)
    q_ref[...] = jnp.dot(hb, w_ref[:, 0:d], preferred_element_type=F32)
    k_ref[...] = jnp.dot(hb, w_ref[:, d:2 * d], preferred_element_type=F32)
    kt_ref[...] = k_ref[...].T
    v_scr[...] = jnp.dot(hb, w_ref[:, 2 * d:3 * d], preferred_element_type=F32)
    vt_ref[...] = v_scr[...].T


def _norm_qkv_t(x2d, g, w, seq_len, layer, n_layers, buffers):
    n = x2d.shape[0]
    tm = min(ROW_TILE, seq_len)
    assert seq_len % tm == 0 and tm % LANES == 0
    tps = seq_len // tm
    row_spec = pl.BlockSpec((tm, D_MODEL), lambda i: (i, 0))
    t_spec = pl.BlockSpec((None, None, D_MODEL, tm), lambda i: (layer, i // tps, 0, i % tps))
    t_shape = jax.ShapeDtypeStruct((n_layers, n // seq_len, D_MODEL, seq_len), F32)
    in_specs = [row_spec, _const_spec((1, D_MODEL)), _const_spec(w.shape)]
    args = [x2d, g.reshape(1, D_MODEL), w]
    aliases = {}
    for k, buf in enumerate(buffers):
        if buf is not None:
            aliases[len(args)] = 2 + k
            in_specs.append(pl.BlockSpec(memory_space=pl.ANY))
            args.append(buf)
    return pl.pallas_call(
        _norm_qkv_t_kernel,
        grid=(n // tm,),
        in_specs=in_specs,
        out_specs=[row_spec, row_spec, t_spec, t_spec],
        out_shape=[jax.ShapeDtypeStruct((n, D_MODEL), F32)] * 2 + [t_shape] * 2,
        scratch_shapes=[pltpu.VMEM((tm, D_MODEL), F32)],
        input_output_aliases=aliases,
        compiler_params=_params(("parallel",)),
        name="norm_qkv_t",
    )(*args)


def _final_norm_kernel(x_ref, g_ref, o_ref):
    o_ref[...] = _rms(x_ref[...], g_ref[...])


def _final_norm(x2d, g):
    n = x2d.shape[0]
    tm = min(2 * ROW_TILE, n)
    return pl.pallas_call(
        _final_norm_kernel,
        grid=(n // tm,),
        in_specs=[pl.BlockSpec((tm, D_MODEL), lambda i: (i, 0)), _const_spec((1, D_MODEL))],
        out_specs=pl.BlockSpec((tm, D_MODEL), lambda i: (i, 0)),
        out_shape=jax.ShapeDtypeStruct((n, D_MODEL), F32),
        compiler_params=_params(("parallel",)),
        name="final_norm",
    )(x2d, g.reshape(1, D_MODEL))


def _post_kernel(*refs, per_seq_prev, tiles_per_seq):
    if per_seq_prev:
        (x_ref, o_ref, wo_ref, g_ref, win_ref, cw_ref, cb_ref, wout_ref, p1_ref, p2_ref,
         y_ref, u_ref) = refs
    else:
        (x_ref, o_ref, wo_ref, g_ref, win_ref, cw_ref, cb_ref, wout_ref,
         y_ref, tail_ref, carry_scr) = refs

        @pl.when(pl.program_id(0) % tiles_per_seq == 0)
        def _():
            carry_scr[...] = jnp.zeros_like(carry_scr)

    tm = x_ref.shape[0]
    x = x_ref[...] + jnp.dot(o_ref[...].astype(BF16), wo_ref[...], preferred_element_type=F32)
    hb = _rms(x, g_ref[...]).astype(BF16)
    if per_seq_prev:
        t_in_seq = lax.broadcasted_iota(jnp.int32, (tm, 1), 0) % SUBLANES
    acc = x
    for c in range(D_FF // FFN_COL_CHUNK):
        conv = []
        for off in (c * FFN_COL_CHUNK, D_FF + c * FFN_COL_CHUNK):
            cols = slice(off, off + FFN_COL_CHUNK)
            u = jnp.dot(hb, win_ref[:, cols], preferred_element_type=F32)
            if per_seq_prev:
                u1 = jnp.where(t_in_seq < 1, p1_ref[:, cols], pltpu.roll(u, 1, 0))
                u2 = jnp.where(t_in_seq < 2, p2_ref[:, cols], pltpu.roll(u, 2, 0))
                u_ref[:, cols] = u
            else:
                tail = carry_scr[:, cols]
                u1 = _shift_rows(u, tail, 1)
                u2 = _shift_rows(u, tail, 2)
                carry_scr[:, cols] = u[tm - SUBLANES:tm]
                tail_ref[:, cols] = u[tm - SUBLANES:tm]
            conv.append(u2 * cw_ref[0:1, cols] + u1 * cw_ref[1:2, cols] + u * cw_ref[2:3, cols]
                        + cb_ref[:, cols])
        act = (_silu(conv[0]) * conv[1]).astype(BF16)
        acc = acc + jnp.dot(act, wout_ref[c * FFN_COL_CHUNK:(c + 1) * FFN_COL_CHUNK, :],
                            preferred_element_type=F32)
    y_ref[...] = acc


def _post(x2d, o2d, w_o, g, w_in, conv_w, conv_b, w_out, seq_len, prev=None):
    n = x2d.shape[0]
    tm = min(ROW_TILE, n)
    nseq = n // seq_len
    row_spec = pl.BlockSpec((tm, D_MODEL), lambda i: (i, 0))
    in_specs = [row_spec, row_spec,
                _const_spec((D_MODEL, D_MODEL)), _const_spec((1, D_MODEL)),
                _const_spec((D_MODEL, 2 * D_FF)), _const_spec((FFN_CONV, 2 * D_FF)),
                _const_spec((1, 2 * D_FF)), _const_spec((D_FF, D_MODEL))]
    args = [x2d, o2d, w_o, g.reshape(1, D_MODEL), w_in, conv_w, conv_b.reshape(1, 2 * D_FF), w_out]
    y_shape = jax.ShapeDtypeStruct((n, D_MODEL), F32)
    if prev is None:
        assert seq_len % tm == 0
        tps = seq_len // tm
        y, tail = pl.pallas_call(
            functools.partial(_post_kernel, per_seq_prev=False, tiles_per_seq=tps),
            grid=(n // tm,),
            in_specs=in_specs,
            out_specs=[row_spec, pl.BlockSpec((None, SUBLANES, 2 * D_FF), lambda i: (i // tps, 0, 0))],
            out_shape=[y_shape, jax.ShapeDtypeStruct((nseq, SUBLANES, 2 * D_FF), F32)],
            scratch_shapes=[pltpu.VMEM((SUBLANES, 2 * D_FF), F32)],
            compiler_params=_params(("arbitrary",)),
            name="post_ffn_prompt",
        )(*args)
        return y, tail[:, SUBLANES - (FFN_CONV - 1):]
    assert seq_len == SUBLANES and n == tm
    pad = ((0, 0), (0, SUBLANES - 1), (0, 0))
    p1 = jnp.pad(prev[:, 1:2], pad).reshape(n, 2 * D_FF)
    p2 = jnp.pad(prev, ((0, 0), (0, SUBLANES - 2), (0, 0))).reshape(n, 2 * D_FF)
    wide_spec = pl.BlockSpec((tm, 2 * D_FF), lambda i: (i, 0))
    y, u = pl.pallas_call(
        functools.partial(_post_kernel, per_seq_prev=True, tiles_per_seq=1),
        grid=(1,),
        in_specs=in_specs + [wide_spec, wide_spec],
        out_specs=[row_spec, wide_spec],
        out_shape=[y_shape, jax.ShapeDtypeStruct((n, 2 * D_FF), F32)],
        compiler_params=_params(("arbitrary",)),
        name="post_ffn_sample",
    )(*args, p1, p2)
    return y, u.reshape(nseq, seq_len, 2 * D_FF)[:, seq_len - (FFN_CONV - 1):]


def _topk_bias_t(gate_t, n_valid, n_rows):
    row = lax.broadcasted_iota(jnp.int32, gate_t.shape, 0)
    valid = row < n_valid
    out = []
    for n in range(n_rows):
        gn = gate_t[n:n + 1, :]
        beats = jnp.where(gate_t > gn, 1.0, jnp.where((gate_t == gn) & (row < n), 1.0, 0.0))
        rank = jnp.sum(jnp.where(valid, beats, 0.0), axis=0, keepdims=True)
        out.append(jnp.where(rank < MOBA_TOPK, 0.0, NEG_INF))
    return out


def _moba_prompt_tile(c, q_ref, o_ref, kb_scr, vt_scr, km_scr):
    blk = MOBA_BLOCK
    hd = A_HEAD_DIM
    q = q_ref[...]
    lane = lax.broadcasted_iota(jnp.int32, (1, LANES), 1)
    causal = (lax.broadcasted_iota(jnp.int32, (blk, blk), 0)
              <= lax.broadcasted_iota(jnp.int32, (blk, blk), 1))
    pair = range(2)
    qh = [jnp.where(lane // hd == hh, q, 0.0) for hh in pair]
    qs = [(x * (hd ** -0.5 * math.log2(math.e))).astype(BF16) for x in qh]
    s = [[lax.dot_general(kb_scr[n], qs[hh], NT_DIMS, preferred_element_type=F32) for n in range(c + 1)]
         for hh in pair]
    for hh in pair:
        s[hh][c] = jnp.where(causal, s[hh][c], NEG_INF)
        if c > MOBA_TOPK:
            gate_t = lax.dot_general(km_scr[...], qh[hh], NT_DIMS, precision=HIGHEST,
                                     preferred_element_type=F32)
            for n, b in enumerate(_topk_bias_t(gate_t, c, c)):
                s[hh][n] = s[hh][n] + b
    m = []
    for hh in pair:
        mv = s[hh][c]
        for n in range(c):
            mv = jnp.maximum(mv, s[hh][n])
        m.append(jnp.max(mv, axis=0, keepdims=True))
    o_t = []
    for hh in pair:
        lv = None
        acc = None
        for n in range(c + 1):
            p = jnp.exp2(s[hh][n] - m[hh])
            lv = p if lv is None else lv + p
            pv = jnp.dot(vt_scr[n, hh * hd:(hh + 1) * hd, :], p.astype(BF16),
                         preferred_element_type=F32)
            acc = pv if acc is None else acc + pv
        o_t.append(acc / jnp.sum(lv, axis=0, keepdims=True))
    o_ref[...] = jnp.concatenate(o_t, axis=0).T.astype(o_ref.dtype)


def _moba_prompt_kernel(q_ref, k_ref, vt_ref, o_ref, kb_scr, vt_scr, km_scr, *, nb):
    i = pl.program_id(2)
    blk = MOBA_BLOCK

    @pl.when(i == 0)
    def _():
        km_scr[...] = jnp.zeros_like(km_scr)
        for n in range(nb):
            kn = k_ref[n * blk:(n + 1) * blk, :]
            kb_scr[n] = kn.astype(BF16)
            vt_scr[n] = vt_ref[:, n * blk:(n + 1) * blk].astype(BF16)
            km_scr[n:n + 1, :] = jnp.mean(kn, axis=0, keepdims=True)

    for c in range(nb):
        pl.when(i == c)(functools.partial(_moba_prompt_tile, c, q_ref, o_ref, kb_scr, vt_scr, km_scr))


def _moba_prompt_attn(q, k, v_t, layer):
    bsz, s, _ = q.shape
    assert s % MOBA_BLOCK == 0
    nb = s // MOBA_BLOCK
    nb_pad = -(-nb // SUBLANES) * SUBLANES
    hp = D_MODEL // LANES
    q_spec = pl.BlockSpec((None, MOBA_BLOCK, LANES), lambda b, h, i: (b, i, h))
    k_spec = pl.BlockSpec((None, s, LANES), lambda b, h, i: (b, 0, h))
    vt_spec = pl.BlockSpec((None, None, LANES, s), lambda b, h, i: (layer, b, h, 0))
    return pl.pallas_call(
        functools.partial(_moba_prompt_kernel, nb=nb),
        grid=(bsz, hp, nb),
        in_specs=[q_spec, k_spec, vt_spec],
        out_specs=q_spec,
        out_shape=jax.ShapeDtypeStruct((bsz, s, D_MODEL), BF16),
        scratch_shapes=[pltpu.VMEM((nb, MOBA_BLOCK, LANES), BF16),
                        pltpu.VMEM((nb, LANES, MOBA_BLOCK), BF16),
                        pltpu.VMEM((nb_pad, LANES), F32)],
        compiler_params=_params(("parallel", "parallel", "arbitrary")),
        name="moba_prompt_attn",
    )(q, k, v_t)


def _topk_select(gate, n_valid, n_cols):
    col = lax.broadcasted_iota(jnp.int32, gate.shape, 1)
    valid = col < n_valid
    sel = []
    for n in range(n_cols):
        gn = gate[:, n:n + 1]
        beats = jnp.where(gate > gn, 1.0, jnp.where((gate == gn) & (col < n), 1.0, 0.0))
        rank = jnp.sum(jnp.where(valid, beats, 0.0), axis=1, keepdims=True)
        sel.append(rank < MOBA_TOPK)
    return sel


def _moba_sample_kernel(pt_ref, q_ref, kn_ref, vn_ref, *refs, n_pages, t_new):
    g_pages = SAMPLE_PAGES_PER_STEP
    k_refs = refs[:g_pages]
    v_refs = refs[g_pages:2 * g_pages]
    o_ref = refs[2 * g_pages]
    qbf_scr, new_scr, gate_scr, sc_scr, l_scr, acc_scr = refs[2 * g_pages + 1:]
    del pt_ref
    j = pl.program_id(1)
    n_steps = n_pages // g_pages
    n_blocks = n_pages * PAGE_SIZE // MOBA_BLOCK
    pages_per_block = MOBA_BLOCK // PAGE_SIZE
    blocks_per_step = g_pages // pages_per_block
    rows = A_HEADS * t_new
    head_of_row = lax.broadcasted_iota(jnp.int32, (rows, D_MODEL), 0) // t_new
    head_of_lane = lax.broadcasted_iota(jnp.int32, (rows, D_MODEL), 1) // A_HEAD_DIM

    @pl.when(j == 0)
    def _():
        qbd = jnp.where(head_of_row == head_of_lane,
                        jnp.concatenate([q_ref[...]] * A_HEADS, axis=0), 0.0)
        qbf_scr[...] = (qbd * (A_HEAD_DIM ** -0.5)).astype(BF16)
        gate_scr[...] = jnp.zeros_like(gate_scr)

    @pl.when(j < n_steps)
    def _():
        col = lax.broadcasted_iota(jnp.int32, gate_scr.shape, 1)
        gate = gate_scr[...]
        for bb in range(blocks_per_step):
            ssum = None
            for g in range(bb * pages_per_block, (bb + 1) * pages_per_block):
                s = jnp.dot(qbf_scr[...], k_refs[g][...].astype(BF16), preferred_element_type=F32)
                sc_scr[j * g_pages + g] = s
                ssum = s if ssum is None else ssum + s
            gmean = jnp.sum(ssum, axis=1, keepdims=True) * (1.0 / MOBA_BLOCK)
            gate = jnp.where(col == j * blocks_per_step + bb, gmean, gate)
        gate_scr[...] = gate

    @pl.when(j == n_steps - 1)
    def _():
        sel = _topk_select(gate_scr[...], n_blocks, n_blocks)
        new_scr[...] = jnp.zeros_like(new_scr)
        new_scr[0, 0:t_new, :] = kn_ref[...].astype(BF16)
        new_scr[1, 0:t_new, :] = vn_ref[...].astype(BF16)
        s_own = lax.dot_general(qbf_scr[...], new_scr[0], NT_DIMS, preferred_element_type=F32)
        r_tok = lax.broadcasted_iota(jnp.int32, s_own.shape, 0) % t_new
        c_tok = lax.broadcasted_iota(jnp.int32, s_own.shape, 1)
        s_own = jnp.where(c_tok <= r_tok, s_own, NEG_INF)
        mv = s_own
        for b in range(n_blocks):
            bias = jnp.broadcast_to(jnp.where(sel[b], 0.0, NEG_INF), s_own.shape)
            for p in range(b * pages_per_block, (b + 1) * pages_per_block):
                mv = jnp.maximum(mv, sc_scr[p] + bias)
        m = jnp.max(mv, axis=1, keepdims=True)
        p_own = jnp.exp(s_own - m)
        lv = p_own
        for b in range(n_blocks):
            shift = jnp.broadcast_to(jnp.where(sel[b], 0.0, NEG_INF) - m, s_own.shape)
            for p in range(b * pages_per_block, (b + 1) * pages_per_block):
                e = jnp.exp(sc_scr[p] + shift)
                lv = lv + e
                sc_scr[p] = e
        l_scr[...] = jnp.sum(lv, axis=1, keepdims=True)
        acc_scr[...] = jnp.dot(p_own.astype(BF16), new_scr[1], preferred_element_type=F32)

    @pl.when(j >= n_steps)
    def _():
        acc = acc_scr[...]
        for g in range(g_pages):
            pp = sc_scr[(j - n_steps) * g_pages + g].astype(BF16)
            acc = acc + lax.dot_general(pp, v_refs[g][...].astype(BF16), NT_DIMS,
                                        preferred_element_type=F32)
        acc_scr[...] = acc

    @pl.when(j == 2 * n_steps - 1)
    def _():
        acc = jnp.where(head_of_row == head_of_lane, acc_scr[...] / l_scr[...], 0.0)
        o = acc[0:t_new]
        for h in range(1, A_HEADS):
            o = o + acc[h * t_new:(h + 1) * t_new]
        o_ref[...] = o


def _moba_sample_attn(q, k_new, v_new, cache_kt, cache_vt, page_table, layer):
    bsz, t_new, _ = q.shape
    n_pages = page_table.shape[1]
    g_pages = SAMPLE_PAGES_PER_STEP
    assert (n_pages * PAGE_SIZE) % MOBA_BLOCK == 0 and MOBA_BLOCK % PAGE_SIZE == 0
    assert (g_pages * PAGE_SIZE) % MOBA_BLOCK == 0 and n_pages % g_pages == 0
    assert t_new == SUBLANES and A_HEADS * t_new == LANES and PAGE_SIZE == LANES
    n_steps = n_pages // g_pages
    n_blocks = n_pages * PAGE_SIZE // MOBA_BLOCK
    assert MOBA_TOPK <= n_blocks <= LANES
    tok_spec = pl.BlockSpec((None, t_new, D_MODEL), lambda b, j, pt: (b, 0, 0))
    new_spec = pl.BlockSpec((None, None, t_new, D_MODEL), lambda b, j, pt: (layer, b, 0, 0))

    def k_spec(g):
        return pl.BlockSpec(
            (None, None, D_MODEL, PAGE_SIZE),
            lambda b, j, pt: (layer, pt[b, g_pages * jnp.minimum(j, n_steps - 1) + g], 0, 0))

    def v_spec(g):
        def index(b, j, pt):
            in_v = j >= n_steps
            seq = jnp.where(in_v, b, jnp.maximum(b - 1, 0))
            step = jnp.where(in_v, j - n_steps, n_steps - 1)
            return (layer, pt[seq, g_pages * step + g], 0, 0)
        return pl.BlockSpec((None, None, D_MODEL, PAGE_SIZE), index)

    rows = A_HEADS * t_new
    grid_spec = pltpu.PrefetchScalarGridSpec(
        num_scalar_prefetch=1,
        grid=(bsz, 2 * n_steps),
        in_specs=[tok_spec, new_spec, new_spec] + [k_spec(g) for g in range(g_pages)]
        + [v_spec(g) for g in range(g_pages)],
        out_specs=tok_spec,
        scratch_shapes=[pltpu.VMEM((rows, D_MODEL), BF16),
                        pltpu.VMEM((2, LANES, D_MODEL), BF16), pltpu.VMEM((rows, LANES), F32),
                        pltpu.VMEM((n_pages, rows, PAGE_SIZE), F32), pltpu.VMEM((rows, 1), F32),
                        pltpu.VMEM((rows, D_MODEL), F32)],
    )
    return pl.pallas_call(
        functools.partial(_moba_sample_kernel, n_pages=n_pages, t_new=t_new),
        grid_spec=grid_spec,
        out_shape=jax.ShapeDtypeStruct((bsz, t_new, D_MODEL), F32),
        compiler_params=_params(("parallel", "arbitrary")),
        name="moba_sample_attn",
    )(page_table, q, k_new, v_new, *([cache_kt] * g_pages), *([cache_vt] * g_pages))


def _gla_kernel(q_ref, k_ref, v_ref, r_ref, glr_ref, wg_ref, bg_ref, ng_ref, s0_ref,
                o_ref, sout_ref, s_scr, *, valid_rows):
    ci = pl.program_id(1)
    c = q_ref.shape[0]

    @pl.when(ci == 0)
    def _():
        s_scr[...] = s0_ref[...]

    x = jnp.dot(glr_ref[...].astype(BF16), wg_ref[...], preferred_element_type=F32) + bg_ref[...]
    gk = -_softplus(-x) * (1.0 / B_GATE_TAU)
    if valid_rows < c:
        gk = jnp.where(lax.broadcasted_iota(jnp.int32, gk.shape, 0) < valid_rows, gk, 0.0)
    b = _cumsum_rows(gk)
    b_last = b[c - 1:c, :]
    k = k_ref[...]
    q_t = (q_ref[...] * (B_DK ** -0.5) * jnp.exp(b)).astype(BF16)
    k_t = (k * jnp.exp(-b)).astype(BF16)
    k_end = (k * jnp.exp(b_last - b)).astype(BF16)
    e_last = jnp.exp(b_last)
    causal = (lax.broadcasted_iota(jnp.int32, (c, c), 1) <= lax.broadcasted_iota(jnp.int32, (c, c), 0))
    for h in range(B_HEADS):
        kc = slice(h * B_DK, (h + 1) * B_DK)
        vc = slice(h * B_DV, (h + 1) * B_DV)
        vh = v_ref[:, vc].astype(BF16)
        st = s_scr[h]
        att = jnp.where(causal, lax.dot_general(q_t[:, kc], k_t[:, kc], NT_DIMS,
                                                preferred_element_type=F32), 0.0)
        o = (jnp.dot(att.astype(BF16), vh, preferred_element_type=F32)
             + lax.dot_general(q_t[:, kc], st.astype(BF16), NT_DIMS, preferred_element_type=F32))
        s_scr[h] = st * e_last[:, kc] + lax.dot_general(vh, k_end[:, kc], TN_DIMS,
                                                        preferred_element_type=F32)
        o = o * lax.rsqrt(jnp.mean(o * o, axis=-1, keepdims=True) + NORM_EPS) * ng_ref[...]
        o_ref[:, vc] = (o * _silu(r_ref[:, vc])).astype(o_ref.dtype)

    @pl.when(ci == pl.num_programs(1) - 1)
    def _():
        sout_ref[...] = s_scr[...]


def _gla_scan(q, k, v, r, glr, w_gate, b_gate, norm_g, s0_t, valid_rows):
    bsz, t, _ = q.shape
    c = B_CHUNK
    assert t % c == 0
    nk = B_HEADS * B_DK

    def tok(width):
        return pl.BlockSpec((None, c, width), lambda b, ci: (b, ci, 0))

    st_spec = pl.BlockSpec((None, B_HEADS, B_DV, B_DK), lambda b, ci: (b, 0, 0, 0))
    return pl.pallas_call(
        functools.partial(_gla_kernel, valid_rows=valid_rows),
        grid=(bsz, t // c),
        in_specs=[tok(nk), tok(nk), tok(D_MODEL), tok(D_MODEL), tok(LANES),
                  _const_spec((LANES, nk)), _const_spec((1, nk)), _const_spec((1, B_DV)), st_spec],
        out_specs=[tok(D_MODEL), st_spec],
        out_shape=[jax.ShapeDtypeStruct((bsz, t, D_MODEL), BF16),
                   jax.ShapeDtypeStruct((bsz, B_HEADS, B_DV, B_DK), F32)],
        scratch_shapes=[pltpu.VMEM((B_HEADS, B_DV, B_DK), F32)],
        compiler_params=_params(("parallel", "arbitrary")),
        name="gla_scan",
    )(q, k, v, r, glr, w_gate, b_gate.reshape(1, nk), norm_g.reshape(1, B_DV), s0_t)


def _split_bf16(x):
    hi = x.astype(BF16)
    return hi, (x - hi.astype(F32)).astype(BF16)


def _dot3(a, b, dims=(((1,), (0,)), ((), ()))):
    ah, al = _split_bf16(a)
    bh, bl = _split_bf16(b)

    def f(x, y):
        return lax.dot_general(x, y, dims, preferred_element_type=F32)
    return f(ah, bh) + (f(ah, bl) + f(al, bh))


def _unit_lower_inverse_minus_eye(a_list):
    c = a_list[0].shape[0]
    x = [-a for a in a_list]
    n = list(x)
    p = 2
    while p < c:
        x = [_dot3(xi, xi) for xi in x]
        n = [ni + xi + _dot3(ni, xi) for ni, xi in zip(n, x)]
        p *= 2
    return n


def _gdn_kernel(qkv_ref, z_ref, ba_ref, cw_ref, alog_ref, dtb_ref, ng_ref, s0_ref,
                o_ref, sout_ref, s_scr, carry_scr, *, valid_lo, valid_hi):
    ci = pl.program_id(1)
    c = qkv_ref.shape[0]
    heads = range(C_HEADS)

    @pl.when(ci == 0)
    def _():
        s_scr[...] = s0_ref[...]
        carry_scr[0:SUBLANES, :] = jnp.zeros((SUBLANES, carry_scr.shape[1]), F32)

    u = qkv_ref[...]
    carry_scr[SUBLANES:SUBLANES + c, :] = u
    conv = u * cw_ref[C_CONV - 1:C_CONV, :]
    for s in range(1, C_CONV):
        conv = conv + carry_scr[SUBLANES - s:SUBLANES - s + c, :] * cw_ref[C_CONV - 1 - s:C_CONV - s, :]
    carry_scr[0:SUBLANES, :] = u[c - SUBLANES:c]
    xc = _silu(conv)

    ba = ba_ref[...]
    beta_all = _sigmoid(ba)
    g_all = -jnp.exp(alog_ref[...]) * _softplus(ba + dtb_ref[...])
    if valid_lo > 0 or valid_hi < c:
        rowi = lax.broadcasted_iota(jnp.int32, ba.shape, 0)
        ok = (rowi >= valid_lo) & (rowi < valid_hi)
        beta_all = jnp.where(ok, beta_all, 0.0)
        g_all = jnp.where(ok, g_all, 0.0)
    gam_all = _cumsum_rows(g_all)
    gam_t = jnp.concatenate([gam_all, jnp.zeros((LANES - c, LANES), F32)], axis=0).T
    ri = lax.broadcasted_iota(jnp.int32, (c, c), 0)
    cj = lax.broadcasted_iota(jnp.int32, (c, c), 1)
    incl = cj <= ri
    strict = cj < ri

    qn, kn, gcol, dec, rhs, a_mat = [], [], [], [], [], []
    for h in heads:
        qh = xc[:, h * C_DK:(h + 1) * C_DK]
        kh = xc[:, D_MODEL + h * C_DK:D_MODEL + (h + 1) * C_DK]
        vh = xc[:, 2 * D_MODEL + h * C_DV:2 * D_MODEL + (h + 1) * C_DV]
        qn.append(qh * lax.rsqrt(jnp.sum(qh * qh, axis=-1, keepdims=True) + NORM_EPS) * (C_DK ** -0.5))
        kn.append(kh * lax.rsqrt(jnp.sum(kh * kh, axis=-1, keepdims=True) + NORM_EPS))
        beta = beta_all[:, h:h + 1]
        gcol.append(gam_all[:, C_HEADS + h:C_HEADS + h + 1])
        grow = gam_t[C_HEADS + h:C_HEADS + h + 1, 0:c]
        dec.append(jnp.where(incl, jnp.exp(jnp.where(incl, gcol[h] - grow, 0.0)), 0.0))
        kb = kn[h] * beta
        rhs.append(jnp.concatenate([kb * jnp.exp(gcol[h]), vh * beta], axis=1))
        a_mat.append(jnp.where(strict, _dot3(kb, kn[h], NT_DIMS) * dec[h], 0.0))
    inv_m = _unit_lower_inverse_minus_eye(a_mat)
    wu = [rhs[h] + _dot3(inv_m[h], rhs[h]) for h in heads]
    att = [_bdot_nt(qn[h], kn[h]) * dec[h] for h in heads]

    st = [s_scr[h] for h in heads]
    stb = [s.astype(BF16) for s in st]
    v_new = [wu[h][:, C_DK:] - lax.dot_general(wu[h][:, :C_DK].astype(BF16), stb[h], NT_DIMS,
                                               preferred_element_type=F32) for h in heads]
    o = [lax.dot_general((qn[h] * jnp.exp(gcol[h])).astype(BF16), stb[h], NT_DIMS,
                         preferred_element_type=F32) + _bdot(att[h], v_new[h]) for h in heads]
    for h in heads:
        g_last = gcol[h][c - 1:c, :]
        s_scr[h] = jnp.exp(g_last) * st[h] + _bdot_tn(v_new[h], kn[h] * jnp.exp(g_last - gcol[h]))
    for h in heads:
        hc = slice(h * C_DV, (h + 1) * C_DV)
        oh = o[h] * lax.rsqrt(jnp.mean(o[h] * o[h], axis=-1, keepdims=True) + NORM_EPS) * ng_ref[...]
        o_ref[:, hc] = (oh * _silu(z_ref[:, hc])).astype(o_ref.dtype)

    @pl.when(ci == pl.num_programs(1) - 1)
    def _():
        sout_ref[...] = s_scr[...]


def _gdn_scan(qkv, z, ba, conv_w, a_log, dt_bias, norm_g, s0_t, valid_lo, valid_hi):
    bsz, t, _ = qkv.shape
    c = C_CHUNK
    assert t % c == 0

    def tok(width):
        return pl.BlockSpec((None, c, width), lambda b, ci: (b, ci, 0))

    def under_a(p):
        return jnp.zeros((1, LANES), F32).at[0, C_HEADS:2 * C_HEADS].set(p)

    st_spec = pl.BlockSpec((None, C_HEADS, C_DV, C_DK), lambda b, ci: (b, 0, 0, 0))
    return pl.pallas_call(
        functools.partial(_gdn_kernel, valid_lo=valid_lo, valid_hi=valid_hi),
        grid=(bsz, t // c),
        in_specs=[tok(3 * D_MODEL), tok(D_MODEL), tok(LANES),
                  _const_spec((C_CONV, 3 * D_MODEL)), _const_spec((1, LANES)),
                  _const_spec((1, LANES)), _const_spec((1, C_DV)), st_spec],
        out_specs=[tok(D_MODEL), st_spec],
        out_shape=[jax.ShapeDtypeStruct((bsz, t, D_MODEL), BF16),
                   jax.ShapeDtypeStruct((bsz, C_HEADS, C_DV, C_DK), F32)],
        scratch_shapes=[pltpu.VMEM((C_HEADS, C_DV, C_DK), F32),
                        pltpu.VMEM((SUBLANES + c, 3 * D_MODEL), F32)],
        compiler_params=_params(("parallel", "arbitrary")),
        name="gdn_scan",
    )(qkv, z, ba, conv_w, under_a(a_log), under_a(dt_bias), norm_g.reshape(1, C_DV), s0_t)


def _pad_cols(w, width):
    return jnp.pad(w, ((0, 0), (0, width - w.shape[1])))


def _pad_tokens(a, before, total):
    return jnp.pad(a, ((0, 0), (before, total - before - a.shape[1]), (0, 0)))


def _moba_mixer(x, g, w_in, layer, n_layers, kv_rows, cache=None):
    bsz, t, _ = x.shape
    x2d = x.reshape(bsz * t, D_MODEL)
    if cache is None:
        q, k, k_t, v_t = _norm_qkv_t(x2d, g, w_in.astype(BF16), t, layer, n_layers, kv_rows)
        o = _moba_prompt_attn(q.reshape(bsz, t, D_MODEL), k.reshape(bsz, t, D_MODEL), v_t, layer)
        return o.reshape(bsz * t, D_MODEL), (k_t, v_t)
    splits = tuple((i * D_MODEL, (i + 1) * D_MODEL) for i in range(3))
    q, k, v = _norm_proj(x2d, g, w_in.astype(BF16), splits, stack=(layer, n_layers, kv_rows))
    k4 = k.reshape(n_layers, bsz, t, D_MODEL)
    v4 = v.reshape(n_layers, bsz, t, D_MODEL)
    o = _moba_sample_attn(q.reshape(bsz, t, D_MODEL), k4, v4, *cache, layer)
    return o.reshape(bsz * t, D_MODEL), (k, v)


def _gla_mixer(x, g, w_in, w_gate, b_gate, norm_g, s0):
    bsz, t, _ = x.shape
    nk = B_HEADS * B_DK
    edges = (0, nk, 2 * nk, 2 * nk + D_MODEL, 2 * nk + 2 * D_MODEL, 2 * nk + 2 * D_MODEL + LANES)
    splits = tuple(zip(edges[:-1], edges[1:]))
    parts = _norm_proj(x.reshape(bsz * t, D_MODEL), g, _pad_cols(w_in, edges[-1]).astype(BF16), splits)
    t_pad = -(-t // B_CHUNK) * B_CHUNK
    q, k, v, r, glr = (_pad_tokens(a.reshape(bsz, t, -1), 0, t_pad) for a in parts)
    wg = jnp.pad(w_gate, ((0, LANES - B_GATE_RANK), (0, 0))).astype(BF16)
    if s0 is None:
        s0_t = jnp.zeros((bsz, B_HEADS, B_DV, B_DK), F32)
    else:
        s0_t = jnp.swapaxes(s0, -1, -2)
    o, s_t = _gla_scan(q, k, v, r, glr, wg, b_gate, norm_g, s0_t, min(t, B_CHUNK))
    return o[:, :t].reshape(bsz * t, D_MODEL), jnp.swapaxes(s_t, -1, -2)


def _gdn_mixer(x, g, w_in, conv_w, a_log, dt_bias, norm_g, s0, conv_prev):
    bsz, t, _ = x.shape
    edges = (0, 3 * D_MODEL, 4 * D_MODEL, 4 * D_MODEL + LANES)
    splits = tuple(zip(edges[:-1], edges[1:]))
    parts = _norm_proj(x.reshape(bsz * t, D_MODEL), g, _pad_cols(w_in, edges[-1]).astype(BF16), splits)
    qkv, z, ba = (a.reshape(bsz, t, -1) for a in parts)
    hist = C_CONV - 1
    if conv_prev is None:
        lo = 0
        conv_new = qkv[:, t - hist:]
        s0_t = jnp.zeros((bsz, C_HEADS, C_DV, C_DK), F32)
        qkv_in = qkv
    else:
        lo = hist
        assert t >= hist
        conv_new = qkv[:, t - hist:]
        s0_t = jnp.swapaxes(s0, -1, -2)
        qkv_in = jnp.concatenate([conv_prev, qkv], axis=1)
    t_pad = -(-(lo + t) // C_CHUNK) * C_CHUNK
    if t_pad != lo + t or lo:
        assert t_pad == C_CHUNK
    qkv_in = _pad_tokens(qkv_in, 0, t_pad)
    z = _pad_tokens(z, lo, t_pad)
    ba = _pad_tokens(ba, lo, t_pad)
    o, s_t = _gdn_scan(qkv_in, z, ba, conv_w, a_log, dt_bias, norm_g, s0_t, lo, min(lo + t, C_CHUNK))
    return o[:, lo:lo + t].reshape(bsz * t, D_MODEL), jnp.swapaxes(s_t, -1, -2), conv_new


def _run_group(x, cache, state_gla, state_gdn, state_gdn_conv, state_ffn_conv, w):
    bsz, t, _ = x.shape
    x2d = x.reshape(bsz * t, D_MODEL)
    out = {name: [] for name in ("gla", "gdn", "gconv", "fconv")}
    n_moba = w["moba_w_in"].shape[0]
    kv_rows = (None, None)
    for i in range(DEPTH):
        j = i // N_MIXERS
        xin = x2d.reshape(bsz, t, D_MODEL)
        if i % N_MIXERS == 0:
            o, kv_rows = _moba_mixer(xin, w["norm_mix"][i], w["moba_w_in"][j], j, n_moba, kv_rows, cache)
            w_o = w["moba_w_out"][j]
        elif i % N_MIXERS == 1:
            s0 = None if state_gla is None else state_gla[j]
            o, s = _gla_mixer(xin, w["norm_mix"][i], w["gla_w_in"][j], w["gla_w_gate"][j],
                              w["gla_b_gate"][j], w["gla_norm"][j], s0)
            out["gla"].append(s)
            w_o = w["gla_w_out"][j]
        else:
            s0 = None if state_gdn is None else state_gdn[j]
            cp = None if state_gdn_conv is None else state_gdn_conv[j]
            o, s, cn = _gdn_mixer(xin, w["norm_mix"][i], w["gdn_w_in"][j], w["gdn_conv_w"][j],
                                  w["gdn_a_log"][j], w["gdn_dt_bias"][j], w["gdn_norm"][j], s0, cp)
            out["gdn"].append(s)
            out["gconv"].append(cn)
            w_o = w["gdn_w_out"][j]
        prev = None if state_ffn_conv is None else state_ffn_conv[i]
        x2d, fc = _post(x2d, o, w_o.astype(BF16), w["norm_ffn"][i], w["ffn_w_in"][i].astype(BF16),
                        w["ffn_conv_w"][i], w["ffn_conv_b"][i], w["ffn_w_out"][i].astype(BF16), t, prev)
        out["fconv"].append(fc)
    y = _final_norm(x2d, w["norm_final"]).reshape(bsz, t, D_MODEL)
    res = {name: jnp.stack(v) for name, v in out.items()}
    if cache is None:
        res["k"], res["v"] = (a.reshape(n_moba, bsz, A_HEADS, A_HEAD_DIM, t).transpose(0, 1, 4, 2, 3)
                              for a in kv_rows)
    else:
        rows = (n_moba, bsz, t, A_HEADS, A_HEAD_DIM)
        res["k"], res["v"] = kv_rows[0].reshape(rows), kv_rows[1].reshape(rows)
    return y, res


def kernel(x_prompt, x_sample, cache_k, cache_v, page_table, state_gla, state_gdn, state_gdn_conv,
           state_ffn_conv, norm_mix, norm_ffn, norm_final, moba_w_in, moba_w_out, gla_w_in, gla_w_gate,
           gla_b_gate, gla_norm, gla_w_out, gdn_w_in, gdn_conv_w, gdn_a_log, gdn_dt_bias, gdn_norm,
           gdn_w_out, ffn_w_in, ffn_conv_w, ffn_conv_b, ffn_w_out):
    w = dict(norm_mix=norm_mix, norm_ffn=norm_ffn, norm_final=norm_final, moba_w_in=moba_w_in,
             moba_w_out=moba_w_out, gla_w_in=gla_w_in, gla_w_gate=gla_w_gate, gla_b_gate=gla_b_gate,
             gla_norm=gla_norm, gla_w_out=gla_w_out, gdn_w_in=gdn_w_in, gdn_conv_w=gdn_conv_w,
             gdn_a_log=gdn_a_log, gdn_dt_bias=gdn_dt_bias, gdn_norm=gdn_norm, gdn_w_out=gdn_w_out,
             ffn_w_in=ffn_w_in, ffn_conv_w=ffn_conv_w, ffn_conv_b=ffn_conv_b, ffn_w_out=ffn_w_out)
    n_layers, n_pool = cache_k.shape[:2]
    pool_shape = (n_layers, n_pool, D_MODEL, PAGE_SIZE)
    cache = (cache_k.transpose(0, 1, 3, 4, 2).reshape(pool_shape),
             cache_v.transpose(0, 1, 3, 4, 2).reshape(pool_shape), page_table)
    yp, p = _run_group(x_prompt, None, None, None, None, None, w)
    ys, s = _run_group(x_sample, cache, state_gla, state_gdn, state_gdn_conv, state_ffn_conv, w)
    return (yp, ys, p["k"], p["v"], s["k"], s["v"], p["gla"], s["gla"], p["gdn"], s["gdn"],
            p["gconv"], s["gconv"], p["fconv"], s["fconv"])
```

```python
import functools
import math

import jax
import jax.numpy as jnp
from jax import lax
from jax.experimental import pallas as pl
from jax.experimental.pallas import tpu as pltpu

F32 = jnp.float32
BF16 = jnp.bfloat16

D_MODEL = 1024
DEPTH = 4
N_MIXERS = 3
NORM_EPS = 1e-6
NEG_INF = -1e30
PAGE_SIZE = 128

A_HEADS = 16
A_HEAD_DIM = D_MODEL // A_HEADS
MOBA_BLOCK = 256
MOBA_TOPK = 3

B_HEADS = 4
B_DK = D_MODEL // 2 // B_HEADS
B_DV = D_MODEL // B_HEADS
B_GATE_RANK = 16
B_GATE_TAU = 16.0
B_CHUNK = 64

C_HEADS = 8
C_DK = D_MODEL // C_HEADS
C_DV = D_MODEL // C_HEADS
C_CONV = 4
C_CHUNK = 64

D_FF = 2816
FFN_CONV = 3

LANES = 128
SUBLANES = 8
ROW_TILE = 512
FFN_COL_CHUNK = 2816
GLA_CHUNKS_PER_STEP = 4
SAMPLE_PAGES_PER_STEP = 16
VMEM_LIMIT = 56 * 1024 * 1024

NT_DIMS = (((1,), (1,)), ((), ()))
TN_DIMS = (((0,), (0,)), ((), ()))


def _params(sem, vmem=VMEM_LIMIT):
    return pltpu.CompilerParams(dimension_semantics=sem, vmem_limit_bytes=vmem)


def _const_spec(shape):
    nd = len(shape)
    return pl.BlockSpec(shape, lambda *_: (0,) * nd, pipeline_mode=pl.Buffered(1))


def _rms(x, g):
    return x * lax.rsqrt(jnp.mean(x * x, axis=-1, keepdims=True) + NORM_EPS) * g


def _sigmoid(x):
    return 1.0 / (1.0 + jnp.exp(-x))


def _silu(x):
    return x * _sigmoid(x)


def _softplus(x):
    return jnp.maximum(x, 0.0) + jnp.log(1.0 + jnp.exp(-jnp.abs(x)))


def _bdot(a, b):
    return jnp.dot(a.astype(BF16), b.astype(BF16), preferred_element_type=F32)


def _bdot_nt(a, b):
    return lax.dot_general(a.astype(BF16), b.astype(BF16), NT_DIMS, preferred_element_type=F32)


def _bdot_tn(a, b):
    return lax.dot_general(a.astype(BF16), b.astype(BF16), TN_DIMS, preferred_element_type=F32)


def _cumsum_rows(x):
    n = x.shape[0]
    row = lax.broadcasted_iota(jnp.int32, x.shape, 0)
    s = 1
    while s < n:
        x = x + jnp.where(row >= s, pltpu.roll(x, s, 0), 0.0)
        s *= 2
    return x


def _shift_rows(u, tail, s):
    r = pltpu.roll(u, s, 0)
    row = lax.broadcasted_iota(jnp.int32, tail.shape, 0)
    head = jnp.where(row < s, pltpu.roll(tail, s, 0), r[0:SUBLANES])
    return jnp.concatenate([head, r[SUBLANES:]], axis=0)


def _norm_proj_kernel(x_ref, g_ref, w_ref, *refs, splits):
    out_refs = refs[len(refs) - len(splits):]
    hb = _rms(x_ref[...], g_ref[...]).astype(BF16)
    for o_ref, (c0, c1) in zip(out_refs, splits):
        o_ref[...] = jnp.dot(hb, w_ref[:, c0:c1], preferred_element_type=F32).astype(o_ref.dtype)


def _norm_proj(x2d, g, w, splits, stack=None):
    n = x2d.shape[0]
    tm = min(ROW_TILE, n)
    out_shape = [jax.ShapeDtypeStruct((n, c1 - c0), F32) for c0, c1 in splits]
    out_specs = [pl.BlockSpec((tm, c1 - c0), lambda i: (i, 0)) for c0, c1 in splits]
    in_specs = [pl.BlockSpec((tm, D_MODEL), lambda i: (i, 0)), _const_spec((1, D_MODEL)),
                _const_spec(w.shape)]
    args = [x2d, g.reshape(1, D_MODEL), w]
    aliases = {}
    if stack is not None:
        layer, n_layers, buffers = stack
        first = len(splits) - len(buffers)
        for k, buf in enumerate(buffers):
            c0, c1 = splits[first + k]
            out_shape[first + k] = jax.ShapeDtypeStruct((n_layers, n, c1 - c0), F32)
            out_specs[first + k] = pl.BlockSpec((None, tm, c1 - c0), lambda i: (layer, i, 0))
            if buf is not None:
                aliases[len(args)] = first + k
                in_specs.append(pl.BlockSpec(memory_space=pl.ANY))
                args.append(buf)
    return pl.pallas_call(
        functools.partial(_norm_proj_kernel, splits=splits),
        grid=(n // tm,),
        in_specs=in_specs,
        out_specs=out_specs,
        out_shape=out_shape,
        input_output_aliases=aliases,
        compiler_params=_params(("parallel",)),
        name="norm_proj",
    )(*args)


def _norm_qkv_t_kernel(x_ref, g_ref, w_ref, *refs):
    q_ref, k_ref, kt_ref, vt_ref, v_scr = refs[len(refs) - 5:]
    d = D_MODEL
    hb = _rms(x_ref[...], g_ref[...]).astype(BF16)
    q_ref[...] = jnp.dot(hb, w_ref[:, 0:d], preferred_element_type=F32)
    k_ref[...] = jnp.dot(hb, w_ref[:, d:2 * d], preferred_element_type=F32)
    kt_ref[...] = k_ref[...].T
    v_scr[...] = jnp.dot(hb, w_ref[:, 2 * d:3 * d], preferred_element_type=F32)
    vt_ref[...] = v_scr[...].T


def _norm_qkv_t(x2d, g, w, seq_len, layer, n_layers, buffers):
    n = x2d.shape[0]
    tm = min(ROW_TILE, seq_len)
    assert seq_len % tm == 0 and tm % LANES == 0
    tps = seq_len // tm
    row_spec = pl.BlockSpec((tm, D_MODEL), lambda i: (i, 0))
    t_spec = pl.BlockSpec((None, None, D_MODEL, tm), lambda i: (layer, i // tps, 0, i % tps))
    t_shape = jax.ShapeDtypeStruct((n_layers, n // seq_len, D_MODEL, seq_len), F32)
    in_specs = [row_spec, _const_spec((1, D_MODEL)), _const_spec(w.shape)]
    args = [x2d, g.reshape(1, D_MODEL), w]
    aliases = {}
    for k, buf in enumerate(buffers):
        if buf is not None:
            aliases[len(args)] = 2 + k
            in_specs.append(pl.BlockSpec(memory_space=pl.ANY))
            args.append(buf)
    return pl.pallas_call(
        _norm_qkv_t_kernel,
        grid=(n // tm,),
        in_specs=in_specs,
        out_specs=[row_spec, row_spec, t_spec, t_spec],
        out_shape=[jax.ShapeDtypeStruct((n, D_MODEL), F32)] * 2 + [t_shape] * 2,
        scratch_shapes=[pltpu.VMEM((tm, D_MODEL), F32)],
        input_output_aliases=aliases,
        compiler_params=_params(("parallel",)),
        name="norm_qkv_t",
    )(*args)


def _post_kernel(*refs, per_seq_prev, tiles_per_seq, final_norm):
    if per_seq_prev:
        (x_ref, o_ref, wo_ref, g_ref, win_ref, cw_ref, cb_ref, wout_ref, gf_ref, p1_ref, p2_ref,
         y_ref, u_ref) = refs
    else:
        (x_ref, o_ref, wo_ref, g_ref, win_ref, cw_ref, cb_ref, wout_ref, gf_ref,
         y_ref, tail_ref, carry_scr) = refs

        @pl.when(pl.program_id(0) % tiles_per_seq == 0)
        def _():
            carry_scr[...] = jnp.zeros_like(carry_scr)

    tm = x_ref.shape[0]
    x = x_ref[...] + jnp.dot(o_ref[...].astype(BF16), wo_ref[...], preferred_element_type=F32)
    hb = _rms(x, g_ref[...]).astype(BF16)
    if per_seq_prev:
        t_in_seq = lax.broadcasted_iota(jnp.int32, (tm, 1), 0) % SUBLANES
    acc = x
    for c in range(D_FF // FFN_COL_CHUNK):
        conv = []
        for off in (c * FFN_COL_CHUNK, D_FF + c * FFN_COL_CHUNK):
            cols = slice(off, off + FFN_COL_CHUNK)
            u = jnp.dot(hb, win_ref[:, cols], preferred_element_type=F32)
            if per_seq_prev:
                u1 = jnp.where(t_in_seq < 1, p1_ref[:, cols], pltpu.roll(u, 1, 0))
                u2 = jnp.where(t_in_seq < 2, p2_ref[:, cols], pltpu.roll(u, 2, 0))
                u_ref[:, cols] = u
            else:
                tail = carry_scr[:, cols]
                u1 = _shift_rows(u, tail, 1)
                u2 = _shift_rows(u, tail, 2)
                carry_scr[:, cols] = u[tm - SUBLANES:tm]
                tail_ref[:, cols] = u[tm - SUBLANES:tm]
            conv.append(u2 * cw_ref[0:1, cols] + u1 * cw_ref[1:2, cols] + u * cw_ref[2:3, cols]
                        + cb_ref[:, cols])
        act = (_silu(conv[0]) * conv[1]).astype(BF16)
        acc = acc + jnp.dot(act, wout_ref[c * FFN_COL_CHUNK:(c + 1) * FFN_COL_CHUNK, :],
                            preferred_element_type=F32)
    y_ref[...] = _rms(acc, gf_ref[...]) if final_norm else acc


def _post(x2d, o2d, w_o, g, w_in, conv_w, conv_b, w_out, g_final, final_norm, seq_len, prev=None):
    n = x2d.shape[0]
    tm = min(ROW_TILE, n)
    nseq = n // seq_len
    row_spec = pl.BlockSpec((tm, D_MODEL), lambda i: (i, 0))
    in_specs = [row_spec, row_spec,
                _const_spec((D_MODEL, D_MODEL)), _const_spec((1, D_MODEL)),
                _const_spec((D_MODEL, 2 * D_FF)), _const_spec((FFN_CONV, 2 * D_FF)),
                _const_spec((1, 2 * D_FF)), _const_spec((D_FF, D_MODEL)), _const_spec((1, D_MODEL))]
    args = [x2d, o2d, w_o, g.reshape(1, D_MODEL), w_in, conv_w, conv_b.reshape(1, 2 * D_FF), w_out,
            g_final.reshape(1, D_MODEL)]
    y_shape = jax.ShapeDtypeStruct((n, D_MODEL), F32)
    if prev is None:
        assert seq_len % tm == 0
        tps = seq_len // tm
        y, tail = pl.pallas_call(
            functools.partial(_post_kernel, per_seq_prev=False, tiles_per_seq=tps, final_norm=final_norm),
            grid=(n // tm,),
            in_specs=in_specs,
            out_specs=[row_spec, pl.BlockSpec((None, SUBLANES, 2 * D_FF), lambda i: (i // tps, 0, 0))],
            out_shape=[y_shape, jax.ShapeDtypeStruct((nseq, SUBLANES, 2 * D_FF), F32)],
            scratch_shapes=[pltpu.VMEM((SUBLANES, 2 * D_FF), F32)],
            compiler_params=_params(("arbitrary",)),
            name="post_ffn_prompt",
        )(*args)
        return y, tail[:, SUBLANES - (FFN_CONV - 1):]
    assert seq_len == SUBLANES and n == tm
    pad = ((0, 0), (0, SUBLANES - 1), (0, 0))
    p1 = jnp.pad(prev[:, 1:2], pad).reshape(n, 2 * D_FF)
    p2 = jnp.pad(prev, ((0, 0), (0, SUBLANES - 2), (0, 0))).reshape(n, 2 * D_FF)
    wide_spec = pl.BlockSpec((tm, 2 * D_FF), lambda i: (i, 0))
    y, u = pl.pallas_call(
        functools.partial(_post_kernel, per_seq_prev=True, tiles_per_seq=1, final_norm=final_norm),
        grid=(1,),
        in_specs=in_specs + [wide_spec, wide_spec],
        out_specs=[row_spec, wide_spec],
        out_shape=[y_shape, jax.ShapeDtypeStruct((n, 2 * D_FF), F32)],
        compiler_params=_params(("arbitrary",)),
        name="post_ffn_sample",
    )(*args, p1, p2)
    return y, u.reshape(nseq, seq_len, 2 * D_FF)[:, seq_len - (FFN_CONV - 1):]


def _topk_bias_t(gate_t, n_valid, n_rows):
    row = lax.broadcasted_iota(jnp.int32, gate_t.shape, 0)
    valid = row < n_valid
    out = []
    for n in range(n_rows):
        gn = gate_t[n:n + 1, :]
        beats = jnp.where(gate_t > gn, 1.0, jnp.where((gate_t == gn) & (row < n), 1.0, 0.0))
        rank = jnp.sum(jnp.where(valid, beats, 0.0), axis=0, keepdims=True)
        out.append(jnp.where(rank < MOBA_TOPK, 0.0, NEG_INF))
    return out


def _moba_prompt_tile(c, q_ref, o_ref, kb_scr, vt_scr, km_scr):
    blk = MOBA_BLOCK
    hd = A_HEAD_DIM
    q = q_ref[...]
    lane = lax.broadcasted_iota(jnp.int32, (1, LANES), 1)
    causal = (lax.broadcasted_iota(jnp.int32, (blk, blk), 0)
              <= lax.broadcasted_iota(jnp.int32, (blk, blk), 1))
    pair = range(2)
    qh = [jnp.where(lane // hd == hh, q, 0.0) for hh in pair]
    qs = [(x * (hd ** -0.5 * math.log2(math.e))).astype(BF16) for x in qh]
    s = [[lax.dot_general(kb_scr[n], qs[hh], NT_DIMS, preferred_element_type=F32) for n in range(c + 1)]
         for hh in pair]
    for hh in pair:
        s[hh][c] = jnp.where(causal, s[hh][c], NEG_INF)
        if c > MOBA_TOPK:
            gate_t = _dot3(km_scr[...], qh[hh], NT_DIMS)
            for n, b in enumerate(_topk_bias_t(gate_t, c, c)):
                s[hh][n] = s[hh][n] + b
    m = []
    for hh in pair:
        mv = s[hh][c]
        for n in range(c):
            mv = jnp.maximum(mv, s[hh][n])
        m.append(jnp.max(mv, axis=0, keepdims=True))
    o_t = []
    for hh in pair:
        lv = None
        acc = None
        for n in range(c + 1):
            p = jnp.exp2(s[hh][n] - m[hh])
            lv = p if lv is None else lv + p
            pv = jnp.dot(vt_scr[n, hh * hd:(hh + 1) * hd, :], p.astype(BF16),
                         preferred_element_type=F32)
            acc = pv if acc is None else acc + pv
        o_t.append(acc / jnp.sum(lv, axis=0, keepdims=True))
    o_ref[...] = jnp.concatenate(o_t, axis=0).T.astype(o_ref.dtype)


def _moba_prompt_kernel(q_ref, k_ref, vt_ref, o_ref, kb_scr, vt_scr, km_scr, *, nb):
    i = pl.program_id(2)
    blk = MOBA_BLOCK

    @pl.when(i == 0)
    def _():
        km_scr[...] = jnp.zeros_like(km_scr)
        for n in range(nb):
            kn = k_ref[n * blk:(n + 1) * blk, :]
            kb_scr[n] = kn.astype(BF16)
            vt_scr[n] = vt_ref[:, n * blk:(n + 1) * blk].astype(BF16)
            km_scr[n:n + 1, :] = jnp.mean(kn, axis=0, keepdims=True)

    for c in range(nb):
        pl.when(i == c)(functools.partial(_moba_prompt_tile, c, q_ref, o_ref, kb_scr, vt_scr, km_scr))


def _moba_prompt_attn(q, k, v_t, layer):
    bsz, s, _ = q.shape
    assert s % MOBA_BLOCK == 0
    nb = s // MOBA_BLOCK
    nb_pad = -(-nb // (2 * SUBLANES)) * 2 * SUBLANES
    hp = D_MODEL // LANES
    q_spec = pl.BlockSpec((None, MOBA_BLOCK, LANES), lambda b, h, i: (b, i, h))
    k_spec = pl.BlockSpec((None, s, LANES), lambda b, h, i: (b, 0, h))
    vt_spec = pl.BlockSpec((None, None, LANES, s), lambda b, h, i: (layer, b, h, 0))
    return pl.pallas_call(
        functools.partial(_moba_prompt_kernel, nb=nb),
        grid=(bsz, hp, nb),
        in_specs=[q_spec, k_spec, vt_spec],
        out_specs=q_spec,
        out_shape=jax.ShapeDtypeStruct((bsz, s, D_MODEL), BF16),
        scratch_shapes=[pltpu.VMEM((nb, MOBA_BLOCK, LANES), BF16),
                        pltpu.VMEM((nb, LANES, MOBA_BLOCK), BF16),
                        pltpu.VMEM((nb_pad, LANES), F32)],
        compiler_params=_params(("parallel", "parallel", "arbitrary")),
        name="moba_prompt_attn",
    )(q, k, v_t)


def _topk_select(gate, n_valid, n_cols):
    col = lax.broadcasted_iota(jnp.int32, gate.shape, 1)
    valid = col < n_valid
    sel = []
    for n in range(n_cols):
        gn = gate[:, n:n + 1]
        beats = jnp.where(gate > gn, 1.0, jnp.where((gate == gn) & (col < n), 1.0, 0.0))
        rank = jnp.sum(jnp.where(valid, beats, 0.0), axis=1, keepdims=True)
        sel.append(rank < MOBA_TOPK)
    return sel


def _moba_sample_kernel(pt_ref, q_ref, kn_ref, vn_ref, *refs, n_pages, t_new):
    g_pages = SAMPLE_PAGES_PER_STEP
    k_refs = refs[:g_pages]
    v_refs = refs[g_pages:2 * g_pages]
    o_ref = refs[2 * g_pages]
    qbf_scr, new_scr, gate_scr, sc_scr, l_scr, acc_scr = refs[2 * g_pages + 1:]
    del pt_ref
    j = pl.program_id(1)
    n_steps = n_pages // g_pages
    n_blocks = n_pages * PAGE_SIZE // MOBA_BLOCK
    pages_per_block = MOBA_BLOCK // PAGE_SIZE
    blocks_per_step = g_pages // pages_per_block
    rows = A_HEADS * t_new
    head_of_row = lax.broadcasted_iota(jnp.int32, (rows, D_MODEL), 0) // t_new
    head_of_lane = lax.broadcasted_iota(jnp.int32, (rows, D_MODEL), 1) // A_HEAD_DIM

    @pl.when(j == 0)
    def _():
        qbd = jnp.where(head_of_row == head_of_lane,
                        jnp.concatenate([q_ref[...]] * A_HEADS, axis=0), 0.0)
        qbf_scr[...] = (qbd * (A_HEAD_DIM ** -0.5)).astype(BF16)
        gate_scr[...] = jnp.zeros_like(gate_scr)

    @pl.when(j < n_steps)
    def _():
        col = lax.broadcasted_iota(jnp.int32, gate_scr.shape, 1)
        gate = gate_scr[...]
        for bb in range(blocks_per_step):
            ssum = None
            for g in range(bb * pages_per_block, (bb + 1) * pages_per_block):
                s = jnp.dot(qbf_scr[...], k_refs[g][...].astype(BF16), preferred_element_type=F32)
                sc_scr[j * g_pages + g] = s
                ssum = s if ssum is None else ssum + s
            gmean = jnp.sum(ssum, axis=1, keepdims=True) * (1.0 / MOBA_BLOCK)
            gate = jnp.where(col == j * blocks_per_step + bb, gmean, gate)
        gate_scr[...] = gate

    @pl.when(j == n_steps - 1)
    def _():
        sel = _topk_select(gate_scr[...], n_blocks, n_blocks)
        new_scr[...] = jnp.zeros_like(new_scr)
        new_scr[0, 0:t_new, :] = kn_ref[...].astype(BF16)
        new_scr[1, 0:t_new, :] = vn_ref[...].astype(BF16)
        s_own = lax.dot_general(qbf_scr[...], new_scr[0], NT_DIMS, preferred_element_type=F32)
        r_tok = lax.broadcasted_iota(jnp.int32, s_own.shape, 0) % t_new
        c_tok = lax.broadcasted_iota(jnp.int32, s_own.shape, 1)
        s_own = jnp.where(c_tok <= r_tok, s_own, NEG_INF)
        mv = s_own
        for b in range(n_blocks):
            bias = jnp.broadcast_to(jnp.where(sel[b], 0.0, NEG_INF), s_own.shape)
            for p in range(b * pages_per_block, (b + 1) * pages_per_block):
                mv = jnp.maximum(mv, sc_scr[p] + bias)
        m = jnp.max(mv, axis=1, keepdims=True)
        p_own = jnp.exp(s_own - m)
        lv = p_own
        for b in range(n_blocks):
            shift = jnp.broadcast_to(jnp.where(sel[b], 0.0, NEG_INF) - m, s_own.shape)
            for p in range(b * pages_per_block, (b + 1) * pages_per_block):
                e = jnp.exp(sc_scr[p] + shift)
                lv = lv + e
                sc_scr[p] = e
        l_scr[...] = jnp.sum(lv, axis=1, keepdims=True)
        acc_scr[...] = jnp.dot(p_own.astype(BF16), new_scr[1], preferred_element_type=F32)

    @pl.when(j >= n_steps)
    def _():
        acc = acc_scr[...]
        for g in range(g_pages):
            pp = sc_scr[(j - n_steps) * g_pages + g].astype(BF16)
            acc = acc + lax.dot_general(pp, v_refs[g][...].astype(BF16), NT_DIMS,
                                        preferred_element_type=F32)
        acc_scr[...] = acc

    @pl.when(j == 2 * n_steps - 1)
    def _():
        acc = jnp.where(head_of_row == head_of_lane, acc_scr[...] / l_scr[...], 0.0)
        o = acc[0:t_new]
        for h in range(1, A_HEADS):
            o = o + acc[h * t_new:(h + 1) * t_new]
        o_ref[...] = o


def _moba_sample_attn(q, k_new, v_new, cache_kt, cache_vt, page_table, layer):
    bsz, t_new, _ = q.shape
    n_pages = page_table.shape[1]
    g_pages = SAMPLE_PAGES_PER_STEP
    assert (n_pages * PAGE_SIZE) % MOBA_BLOCK == 0 and MOBA_BLOCK % PAGE_SIZE == 0
    assert (g_pages * PAGE_SIZE) % MOBA_BLOCK == 0 and n_pages % g_pages == 0
    assert t_new == SUBLANES and A_HEADS * t_new == LANES and PAGE_SIZE == LANES
    n_steps = n_pages // g_pages
    n_blocks = n_pages * PAGE_SIZE // MOBA_BLOCK
    assert MOBA_TOPK <= n_blocks <= LANES
    tok_spec = pl.BlockSpec((None, t_new, D_MODEL), lambda b, j, pt: (b, 0, 0))
    new_spec = pl.BlockSpec((None, None, t_new, D_MODEL), lambda b, j, pt: (layer, b, 0, 0))

    def k_spec(g):
        return pl.BlockSpec(
            (None, None, D_MODEL, PAGE_SIZE),
            lambda b, j, pt: (layer, pt[b, g_pages * jnp.minimum(j, n_steps - 1) + g], 0, 0))

    def v_spec(g):
        def index(b, j, pt):
            in_v = j >= n_steps
            seq = jnp.where(in_v, b, jnp.maximum(b - 1, 0))
            step = jnp.where(in_v, j - n_steps, n_steps - 1)
            return (layer, pt[seq, g_pages * step + g], 0, 0)
        return pl.BlockSpec((None, None, D_MODEL, PAGE_SIZE), index)

    rows = A_HEADS * t_new
    grid_spec = pltpu.PrefetchScalarGridSpec(
        num_scalar_prefetch=1,
        grid=(bsz, 2 * n_steps),
        in_specs=[tok_spec, new_spec, new_spec] + [k_spec(g) for g in range(g_pages)]
        + [v_spec(g) for g in range(g_pages)],
        out_specs=tok_spec,
        scratch_shapes=[pltpu.VMEM((rows, D_MODEL), BF16),
                        pltpu.VMEM((2, LANES, D_MODEL), BF16), pltpu.VMEM((rows, LANES), F32),
                        pltpu.VMEM((n_pages, rows, PAGE_SIZE), F32), pltpu.VMEM((rows, 1), F32),
                        pltpu.VMEM((rows, D_MODEL), F32)],
    )
    return pl.pallas_call(
        functools.partial(_moba_sample_kernel, n_pages=n_pages, t_new=t_new),
        grid_spec=grid_spec,
        out_shape=jax.ShapeDtypeStruct((bsz, t_new, D_MODEL), F32),
        compiler_params=_params(("parallel", "arbitrary")),
        name="moba_sample_attn",
    )(page_table, q, k_new, v_new, *([cache_kt] * g_pages), *([cache_vt] * g_pages))


def _gla_kernel(q_ref, k_ref, v_ref, r_ref, glr_ref, wg_ref, bg_ref, ng_ref, s0_ref,
                o_ref, sout_ref, s_scr, *, valid_rows):
    ci = pl.program_id(1)
    c = B_CHUNK
    n_sub = q_ref.shape[0] // c

    @pl.when(ci == 0)
    def _():
        s_scr[...] = s0_ref[...]

    causal = (lax.broadcasted_iota(jnp.int32, (c, c), 1) <= lax.broadcasted_iota(jnp.int32, (c, c), 0))
    st = [s_scr[h] for h in range(B_HEADS)]
    for j in range(n_sub):
        rows = slice(j * c, (j + 1) * c)
        x = (jnp.dot(glr_ref[rows, :].astype(BF16), wg_ref[...], preferred_element_type=F32)
             + bg_ref[...])
        gk = -_softplus(-x) * (1.0 / B_GATE_TAU)
        if valid_rows < c:
            gk = jnp.where(lax.broadcasted_iota(jnp.int32, gk.shape, 0) < valid_rows, gk, 0.0)
        b = _cumsum_rows(gk)
        b_last = b[c - 1:c, :]
        k = k_ref[rows, :]
        q_t = (q_ref[rows, :] * (B_DK ** -0.5) * jnp.exp(b)).astype(BF16)
        k_t = (k * jnp.exp(-b)).astype(BF16)
        k_end = (k * jnp.exp(b_last - b)).astype(BF16)
        e_last = jnp.exp(b_last)
        for h in range(B_HEADS):
            kc = slice(h * B_DK, (h + 1) * B_DK)
            vc = slice(h * B_DV, (h + 1) * B_DV)
            vh = v_ref[rows, vc].astype(BF16)
            att = jnp.where(causal, lax.dot_general(q_t[:, kc], k_t[:, kc], NT_DIMS,
                                                    preferred_element_type=F32), 0.0)
            o = (jnp.dot(att.astype(BF16), vh, preferred_element_type=F32)
                 + lax.dot_general(q_t[:, kc], st[h].astype(BF16), NT_DIMS,
                                   preferred_element_type=F32))
            st[h] = st[h] * e_last[:, kc] + lax.dot_general(vh, k_end[:, kc], TN_DIMS,
                                                            preferred_element_type=F32)
            o = o * lax.rsqrt(jnp.mean(o * o, axis=-1, keepdims=True) + NORM_EPS) * ng_ref[...]
            o_ref[rows, vc] = (o * _silu(r_ref[rows, vc])).astype(o_ref.dtype)
    for h in range(B_HEADS):
        s_scr[h] = st[h]

    @pl.when(ci == pl.num_programs(1) - 1)
    def _():
        sout_ref[...] = s_scr[...]


def _gla_scan(q, k, v, r, glr, w_gate, b_gate, norm_g, s0_t, valid_rows):
    bsz, t, _ = q.shape
    assert t % B_CHUNK == 0
    rows = math.gcd(t, GLA_CHUNKS_PER_STEP * B_CHUNK)
    assert valid_rows == B_CHUNK or rows == B_CHUNK
    nk = B_HEADS * B_DK

    def tok(width):
        return pl.BlockSpec((None, rows, width), lambda b, ci: (b, ci, 0))

    st_spec = pl.BlockSpec((None, B_HEADS, B_DV, B_DK), lambda b, ci: (b, 0, 0, 0))
    return pl.pallas_call(
        functools.partial(_gla_kernel, valid_rows=valid_rows),
        grid=(bsz, t // rows),
        in_specs=[tok(nk), tok(nk), tok(D_MODEL), tok(D_MODEL), tok(LANES),
                  _const_spec((LANES, nk)), _const_spec((1, nk)), _const_spec((1, B_DV)), st_spec],
        out_specs=[tok(D_MODEL), st_spec],
        out_shape=[jax.ShapeDtypeStruct((bsz, t, D_MODEL), BF16),
                   jax.ShapeDtypeStruct((bsz, B_HEADS, B_DV, B_DK), F32)],
        scratch_shapes=[pltpu.VMEM((B_HEADS, B_DV, B_DK), F32)],
        compiler_params=_params(("parallel", "arbitrary")),
        name="gla_scan",
    )(q, k, v, r, glr, w_gate, b_gate.reshape(1, nk), norm_g.reshape(1, B_DV), s0_t)


def _split_bf16(x):
    hi = x.astype(BF16)
    return hi, (x - hi.astype(F32)).astype(BF16)


def _dot3(a, b, dims=(((1,), (0,)), ((), ()))):
    ah, al = _split_bf16(a)
    bh, bl = _split_bf16(b)

    def f(x, y):
        return lax.dot_general(x, y, dims, preferred_element_type=F32)
    return f(ah, bh) + (f(ah, bl) + f(al, bh))


def _unit_lower_inverse_minus_eye(a_list):
    c = a_list[0].shape[0]
    base = SUBLANES
    assert c % base == 0 and (c // base) & (c // base - 1) == 0
    ri = lax.broadcasted_iota(jnp.int32, (c, c), 0)
    cj = lax.broadcasted_iota(jnp.int32, (c, c), 1)
    x = [jnp.where(ri // base == cj // base, -a, 0.0) for a in a_list]
    n = list(x)
    p = 2
    while p < base:
        x = [_dot3(xi, xi) for xi in x]
        n = [ni + xi + _dot3(ni, xi) for ni, xi in zip(n, x)]
        p *= 2
    m = base
    while m < c:
        pair = (ri // (2 * m) == cj // (2 * m)) & (ri // m != cj // m)
        low = [jnp.where(pair, a, 0.0) for a in a_list]
        d_l = [li + _dot3(ni, li) for ni, li in zip(n, low)]
        n = [ni - dli - _dot3(dli, ni) for ni, dli in zip(n, d_l)]
        m *= 2
    return n


def _gdn_kernel(qkv_ref, z_ref, ba_ref, cw_ref, alog_ref, dtb_ref, ng_ref, s0_ref,
                o_ref, sout_ref, s_scr, carry_scr, *, valid_lo, valid_hi):
    ci = pl.program_id(1)
    c = qkv_ref.shape[0]
    heads = range(C_HEADS)

    @pl.when(ci == 0)
    def _():
        s_scr[...] = s0_ref[...]
        carry_scr[0:SUBLANES, :] = jnp.zeros((SUBLANES, carry_scr.shape[1]), F32)

    u = qkv_ref[...]
    carry_scr[SUBLANES:SUBLANES + c, :] = u
    conv = u * cw_ref[C_CONV - 1:C_CONV, :]
    for s in range(1, C_CONV):
        conv = conv + carry_scr[SUBLANES - s:SUBLANES - s + c, :] * cw_ref[C_CONV - 1 - s:C_CONV - s, :]
    carry_scr[0:SUBLANES, :] = u[c - SUBLANES:c]
    xc = _silu(conv)

    ba = ba_ref[...]
    beta_all = _sigmoid(ba)
    g_all = -jnp.exp(alog_ref[...]) * _softplus(ba + dtb_ref[...])
    if valid_lo > 0 or valid_hi < c:
        rowi = lax.broadcasted_iota(jnp.int32, ba.shape, 0)
        ok = (rowi >= valid_lo) & (rowi < valid_hi)
        beta_all = jnp.where(ok, beta_all, 0.0)
        g_all = jnp.where(ok, g_all, 0.0)
    gam_all = _cumsum_rows(g_all)
    gam_t = jnp.concatenate([gam_all, jnp.zeros((LANES - c, LANES), F32)], axis=0).T
    ri = lax.broadcasted_iota(jnp.int32, (c, c), 0)
    cj = lax.broadcasted_iota(jnp.int32, (c, c), 1)
    incl = cj <= ri
    strict = cj < ri

    qn, kn, gcol, dec, rhs, a_mat = [], [], [], [], [], []
    for h in heads:
        qh = xc[:, h * C_DK:(h + 1) * C_DK]
        kh = xc[:, D_MODEL + h * C_DK:D_MODEL + (h + 1) * C_DK]
        vh = xc[:, 2 * D_MODEL + h * C_DV:2 * D_MODEL + (h + 1) * C_DV]
        qn.append(qh * lax.rsqrt(jnp.sum(qh * qh, axis=-1, keepdims=True) + NORM_EPS) * (C_DK ** -0.5))
        kn.append(kh * lax.rsqrt(jnp.sum(kh * kh, axis=-1, keepdims=True) + NORM_EPS))
        beta = beta_all[:, h:h + 1]
        gcol.append(gam_all[:, C_HEADS + h:C_HEADS + h + 1])
        grow = gam_t[C_HEADS + h:C_HEADS + h + 1, 0:c]
        dec.append(jnp.where(incl, jnp.exp(jnp.where(incl, gcol[h] - grow, 0.0)), 0.0))
        kb = kn[h] * beta
        rhs.append(jnp.concatenate([kb * jnp.exp(gcol[h]), vh * beta], axis=1))
        a_mat.append(jnp.where(strict, _dot3(kb, kn[h], NT_DIMS) * dec[h], 0.0))
    inv_m = _unit_lower_inverse_minus_eye(a_mat)
    wu = [rhs[h] + _dot3(inv_m[h], rhs[h]) for h in heads]
    att = [_bdot_nt(qn[h], kn[h]) * dec[h] for h in heads]

    st = [s_scr[h] for h in heads]
    stb = [s.astype(BF16) for s in st]
    v_new = [wu[h][:, C_DK:] - lax.dot_general(wu[h][:, :C_DK].astype(BF16), stb[h], NT_DIMS,
                                               preferred_element_type=F32) for h in heads]
    o = [lax.dot_general((qn[h] * jnp.exp(gcol[h])).astype(BF16), stb[h], NT_DIMS,
                         preferred_element_type=F32) + _bdot(att[h], v_new[h]) for h in heads]
    for h in heads:
        g_last = gcol[h][c - 1:c, :]
        s_scr[h] = jnp.exp(g_last) * st[h] + _bdot_tn(v_new[h], kn[h] * jnp.exp(g_last - gcol[h]))
    for h in heads:
        hc = slice(h * C_DV, (h + 1) * C_DV)
        oh = o[h] * lax.rsqrt(jnp.mean(o[h] * o[h], axis=-1, keepdims=True) + NORM_EPS) * ng_ref[...]
        o_ref[:, hc] = (oh * _silu(z_ref[:, hc])).astype(o_ref.dtype)

    @pl.when(ci == pl.num_programs(1) - 1)
    def _():
        sout_ref[...] = s_scr[...]


def _gdn_scan(qkv, z, ba, conv_w, a_log, dt_bias, norm_g, s0_t, valid_lo, valid_hi):
    bsz, t, _ = qkv.shape
    c = C_CHUNK
    assert t % c == 0

    def tok(width):
        return pl.BlockSpec((None, c, width), lambda b, ci: (b, ci, 0))

    def under_a(p):
        return jnp.zeros((1, LANES), F32).at[0, C_HEADS:2 * C_HEADS].set(p)

    st_spec = pl.BlockSpec((None, C_HEADS, C_DV, C_DK), lambda b, ci: (b, 0, 0, 0))
    return pl.pallas_call(
        functools.partial(_gdn_kernel, valid_lo=valid_lo, valid_hi=valid_hi),
        grid=(bsz, t // c),
        in_specs=[tok(3 * D_MODEL), tok(D_MODEL), tok(LANES),
                  _const_spec((C_CONV, 3 * D_MODEL)), _const_spec((1, LANES)),
                  _const_spec((1, LANES)), _const_spec((1, C_DV)), st_spec],
        out_specs=[tok(D_MODEL), st_spec],
        out_shape=[jax.ShapeDtypeStruct((bsz, t, D_MODEL), BF16),
                   jax.ShapeDtypeStruct((bsz, C_HEADS, C_DV, C_DK), F32)],
        scratch_shapes=[pltpu.VMEM((C_HEADS, C_DV, C_DK), F32),
                        pltpu.VMEM((SUBLANES + c, 3 * D_MODEL), F32)],
        compiler_params=_params(("parallel", "arbitrary")),
        name="gdn_scan",
    )(qkv, z, ba, conv_w, under_a(a_log), under_a(dt_bias), norm_g.reshape(1, C_DV), s0_t)


def _pad_cols(w, width):
    return jnp.pad(w, ((0, 0), (0, width - w.shape[1])))


def _pad_tokens(a, before, total):
    return jnp.pad(a, ((0, 0), (before, total - before - a.shape[1]), (0, 0)))


def _moba_mixer(x, g, w_in, layer, n_layers, kv_rows, cache=None):
    bsz, t, _ = x.shape
    x2d = x.reshape(bsz * t, D_MODEL)
    if cache is None:
        q, k, k_t, v_t = _norm_qkv_t(x2d, g, w_in.astype(BF16), t, layer, n_layers, kv_rows)
        o = _moba_prompt_attn(q.reshape(bsz, t, D_MODEL), k.reshape(bsz, t, D_MODEL), v_t, layer)
        return o.reshape(bsz * t, D_MODEL), (k_t, v_t)
    splits = tuple((i * D_MODEL, (i + 1) * D_MODEL) for i in range(3))
    q, k, v = _norm_proj(x2d, g, w_in.astype(BF16), splits, stack=(layer, n_layers, kv_rows))
    k4 = k.reshape(n_layers, bsz, t, D_MODEL)
    v4 = v.reshape(n_layers, bsz, t, D_MODEL)
    o = _moba_sample_attn(q.reshape(bsz, t, D_MODEL), k4, v4, *cache, layer)
    return o.reshape(bsz * t, D_MODEL), (k, v)


def _gla_mixer(x, g, w_in, w_gate, b_gate, norm_g, s0):
    bsz, t, _ = x.shape
    nk = B_HEADS * B_DK
    edges = (0, nk, 2 * nk, 2 * nk + D_MODEL, 2 * nk + 2 * D_MODEL, 2 * nk + 2 * D_MODEL + LANES)
    splits = tuple(zip(edges[:-1], edges[1:]))
    parts = _norm_proj(x.reshape(bsz * t, D_MODEL), g, _pad_cols(w_in, edges[-1]).astype(BF16), splits)
    t_pad = -(-t // B_CHUNK) * B_CHUNK
    q, k, v, r, glr = (_pad_tokens(a.reshape(bsz, t, -1), 0, t_pad) for a in parts)
    wg = jnp.pad(w_gate, ((0, LANES - B_GATE_RANK), (0, 0))).astype(BF16)
    if s0 is None:
        s0_t = jnp.zeros((bsz, B_HEADS, B_DV, B_DK), F32)
    else:
        s0_t = jnp.swapaxes(s0, -1, -2)
    o, s_t = _gla_scan(q, k, v, r, glr, wg, b_gate, norm_g, s0_t, min(t, B_CHUNK))
    return o[:, :t].reshape(bsz * t, D_MODEL), jnp.swapaxes(s_t, -1, -2)


def _gdn_mixer(x, g, w_in, conv_w, a_log, dt_bias, norm_g, s0, conv_prev):
    bsz, t, _ = x.shape
    edges = (0, 3 * D_MODEL, 4 * D_MODEL, 4 * D_MODEL + LANES)
    splits = tuple(zip(edges[:-1], edges[1:]))
    parts = _norm_proj(x.reshape(bsz * t, D_MODEL), g, _pad_cols(w_in, edges[-1]).astype(BF16), splits)
    qkv, z, ba = (a.reshape(bsz, t, -1) for a in parts)
    hist = C_CONV - 1
    if conv_prev is None:
        lo = 0
        conv_new = qkv[:, t - hist:]
        s0_t = jnp.zeros((bsz, C_HEADS, C_DV, C_DK), F32)
        qkv_in = qkv
    else:
        lo = hist
        assert t >= hist
        conv_new = qkv[:, t - hist:]
        s0_t = jnp.swapaxes(s0, -1, -2)
        qkv_in = jnp.concatenate([conv_prev, qkv], axis=1)
    t_pad = -(-(lo + t) // C_CHUNK) * C_CHUNK
    if t_pad != lo + t or lo:
        assert t_pad == C_CHUNK
    qkv_in = _pad_tokens(qkv_in, 0, t_pad)
    z = _pad_tokens(z, lo, t_pad)
    ba = _pad_tokens(ba, lo, t_pad)
    o, s_t = _gdn_scan(qkv_in, z, ba, conv_w, a_log, dt_bias, norm_g, s0_t, lo, min(lo + t, C_CHUNK))
    return o[:, lo:lo + t].reshape(bsz * t, D_MODEL), jnp.swapaxes(s_t, -1, -2), conv_new


def _run_group(x, cache, state_gla, state_gdn, state_gdn_conv, state_ffn_conv, w):
    bsz, t, _ = x.shape
    x2d = x.reshape(bsz * t, D_MODEL)
    out = {name: [] for name in ("gla", "gdn", "gconv", "fconv")}
    n_moba = w["moba_w_in"].shape[0]
    kv_rows = (None, None)
    for i in range(DEPTH):
        j = i // N_MIXERS
        xin = x2d.reshape(bsz, t, D_MODEL)
        if i % N_MIXERS == 0:
            o, kv_rows = _moba_mixer(xin, w["norm_mix"][i], w["moba_w_in"][j], j, n_moba, kv_rows, cache)
            w_o = w["moba_w_out"][j]
        elif i % N_MIXERS == 1:
            s0 = None if state_gla is None else state_gla[j]
            o, s = _gla_mixer(xin, w["norm_mix"][i], w["gla_w_in"][j], w["gla_w_gate"][j],
                              w["gla_b_gate"][j], w["gla_norm"][j], s0)
            out["gla"].append(s)
            w_o = w["gla_w_out"][j]
        else:
            s0 = None if state_gdn is None else state_gdn[j]
            cp = None if state_gdn_conv is None else state_gdn_conv[j]
            o, s, cn = _gdn_mixer(xin, w["norm_mix"][i], w["gdn_w_in"][j], w["gdn_conv_w"][j],
                                  w["gdn_a_log"][j], w["gdn_dt_bias"][j], w["gdn_norm"][j], s0, cp)
            out["gdn"].append(s)
            out["gconv"].append(cn)
            w_o = w["gdn_w_out"][j]
        prev = None if state_ffn_conv is None else state_ffn_conv[i]
        x2d, fc = _post(x2d, o, w_o.astype(BF16), w["norm_ffn"][i], w["ffn_w_in"][i].astype(BF16),
                        w["ffn_conv_w"][i], w["ffn_conv_b"][i], w["ffn_w_out"][i].astype(BF16),
                        w["norm_final"], i == DEPTH - 1, t, prev)
        out["fconv"].append(fc)
    y = x2d.reshape(bsz, t, D_MODEL)
    res = {name: jnp.stack(v) for name, v in out.items()}
    if cache is None:
        res["k"], res["v"] = (a.reshape(n_moba, bsz, A_HEADS, A_HEAD_DIM, t).transpose(0, 1, 4, 2, 3)
                              for a in kv_rows)
    else:
        rows = (n_moba, bsz, t, A_HEADS, A_HEAD_DIM)
        res["k"], res["v"] = kv_rows[0].reshape(rows), kv_rows[1].reshape(rows)
    return y, res


def kernel(x_prompt, x_sample, cache_k, cache_v, page_table, state_gla, state_gdn, state_gdn_conv,
           state_ffn_conv, norm_mix, norm_ffn, norm_final, moba_w_in, moba_w_out, gla_w_in, gla_w_gate,
           gla_b_gate, gla_norm, gla_w_out, gdn_w_in, gdn_conv_w, gdn_a_log, gdn_dt_bias, gdn_norm,
           gdn_w_out, ffn_w_in, ffn_conv_w, ffn_conv_b, ffn_w_out):
    w = dict(norm_mix=norm_mix, norm_ffn=norm_ffn, norm_final=norm_final, moba_w_in=moba_w_in,
             moba_w_out=moba_w_out, gla_w_in=gla_w_in, gla_w_gate=gla_w_gate, gla_b_gate=gla_b_gate,
             gla_norm=gla_norm, gla_w_out=gla_w_out, gdn_w_in=gdn_w_in, gdn_conv_w=gdn_conv_w,
             gdn_a_log=gdn_a_log, gdn_dt_bias=gdn_dt_bias, gdn_norm=gdn_norm, gdn_w_out=gdn_w_out,
             ffn_w_in=ffn_w_in, ffn_conv_w=ffn_conv_w, ffn_conv_b=ffn_conv_b, ffn_w_out=ffn_w_out)
    n_layers, n_pool = cache_k.shape[:2]
    pool_shape = (n_layers, n_pool, D_MODEL, PAGE_SIZE)
    cache = (cache_k.transpose(0, 1, 3, 4, 2).reshape(pool_shape),
             cache_v.transpose(0, 1, 3, 4, 2).reshape(pool_shape), page_table)
    yp, p = _run_group(x_prompt, None, None, None, None, None, w)
    ys, s = _run_group(x_sample, cache, state_gla, state_gdn, state_gdn_conv, state_ffn_conv, w)
    return (yp, ys, p["k"], p["v"], s["k"], s["v"], p["gla"], s["gla"], p["gdn"], s["gdn"],
            p["gconv"], s["gconv"], p["fconv"], s["fconv"])
```

```python
import functools
import math

import jax
import jax.numpy as jnp
from jax import lax
from jax.experimental import pallas as pl
from jax.experimental.pallas import tpu as pltpu

F32 = jnp.float32
BF16 = jnp.bfloat16

D_MODEL = 1024
DEPTH = 4
N_MIXERS = 3
NORM_EPS = 1e-6
NEG_INF = -1e30
PAGE_SIZE = 128

A_HEADS = 16
A_HEAD_DIM = D_MODEL // A_HEADS
MOBA_BLOCK = 256
MOBA_TOPK = 3

B_HEADS = 4
B_DK = D_MODEL // 2 // B_HEADS
B_DV = D_MODEL // B_HEADS
B_GATE_RANK = 16
B_GATE_TAU = 16.0
B_CHUNK = 64

C_HEADS = 8
C_DK = D_MODEL // C_HEADS
C_DV = D_MODEL // C_HEADS
C_CONV = 4
C_CHUNK = 64

D_FF = 2816
FFN_CONV = 3

LANES = 128
SUBLANES = 8
ROW_TILE = 512
FFN_COL_CHUNK = 2816
GLA_CHUNKS_PER_STEP = 8
SAMPLE_PAGES_PER_STEP = 16
VMEM_LIMIT = 56 * 1024 * 1024

NT_DIMS = (((1,), (1,)), ((), ()))
TN_DIMS = (((0,), (0,)), ((), ()))


def _params(sem, vmem=VMEM_LIMIT):
    return pltpu.CompilerParams(dimension_semantics=sem, vmem_limit_bytes=vmem)


def _const_spec(shape):
    nd = len(shape)
    return pl.BlockSpec(shape, lambda *_: (0,) * nd, pipeline_mode=pl.Buffered(1))


def _rms(x, g):
    return x * lax.rsqrt(jnp.mean(x * x, axis=-1, keepdims=True) + NORM_EPS) * g


def _sigmoid(x):
    return 1.0 / (1.0 + jnp.exp(-x))


def _silu(x):
    return x * _sigmoid(x)


def _softplus(x):
    return jnp.maximum(x, 0.0) + jnp.log(1.0 + jnp.exp(-jnp.abs(x)))


def _bdot(a, b):
    return jnp.dot(a.astype(BF16), b.astype(BF16), preferred_element_type=F32)


def _bdot_nt(a, b):
    return lax.dot_general(a.astype(BF16), b.astype(BF16), NT_DIMS, preferred_element_type=F32)


def _bdot_tn(a, b):
    return lax.dot_general(a.astype(BF16), b.astype(BF16), TN_DIMS, preferred_element_type=F32)


def _cumsum_rows(x):
    n = x.shape[0]
    row = lax.broadcasted_iota(jnp.int32, x.shape, 0)
    s = 1
    while s < n:
        x = x + jnp.where(row >= s, pltpu.roll(x, s, 0), 0.0)
        s *= 2
    return x


def _shift_rows(u, tail, s):
    r = pltpu.roll(u, s, 0)
    row = lax.broadcasted_iota(jnp.int32, tail.shape, 0)
    head = jnp.where(row < s, pltpu.roll(tail, s, 0), r[0:SUBLANES])
    return jnp.concatenate([head, r[SUBLANES:]], axis=0)


def _norm_proj_kernel(x_ref, g_ref, w_ref, *refs, splits):
    out_refs = refs[len(refs) - len(splits):]
    hb = _rms(x_ref[...], g_ref[...]).astype(BF16)
    for o_ref, (c0, c1) in zip(out_refs, splits):
        o_ref[...] = jnp.dot(hb, w_ref[:, c0:c1], preferred_element_type=F32).astype(o_ref.dtype)


def _norm_proj(x2d, g, w, splits, stack=None):
    n = x2d.shape[0]
    tm = min(ROW_TILE, n)
    out_shape = [jax.ShapeDtypeStruct((n, c1 - c0), F32) for c0, c1 in splits]
    out_specs = [pl.BlockSpec((tm, c1 - c0), lambda i: (i, 0)) for c0, c1 in splits]
    in_specs = [pl.BlockSpec((tm, D_MODEL), lambda i: (i, 0)), _const_spec((1, D_MODEL)),
                _const_spec(w.shape)]
    args = [x2d, g.reshape(1, D_MODEL), w]
    aliases = {}
    if stack is not None:
        layer, n_layers, buffers = stack
        first = len(splits) - len(buffers)
        for k, buf in enumerate(buffers):
            c0, c1 = splits[first + k]
            out_shape[first + k] = jax.ShapeDtypeStruct((n_layers, n, c1 - c0), F32)
            out_specs[first + k] = pl.BlockSpec((None, tm, c1 - c0), lambda i: (layer, i, 0))
            if buf is not None:
                aliases[len(args)] = first + k
                in_specs.append(pl.BlockSpec(memory_space=pl.ANY))
                args.append(buf)
    return pl.pallas_call(
        functools.partial(_norm_proj_kernel, splits=splits),
        grid=(n // tm,),
        in_specs=in_specs,
        out_specs=out_specs,
        out_shape=out_shape,
        input_output_aliases=aliases,
        compiler_params=_params(("parallel",)),
        name="norm_proj",
    )(*args)


def _norm_qkv_t_kernel(x_ref, g_ref, w_ref, *refs, layer, fresh):
    q_ref, k_ref, kt_ref, vt_ref, v_scr = refs[len(refs) - 5:]
    d = D_MODEL
    hb = _rms(x_ref[...], g_ref[...]).astype(BF16)
    q_ref[...] = jnp.dot(hb, w_ref[:, 0:d], preferred_element_type=F32)
    k_ref[...] = jnp.dot(hb, w_ref[:, d:2 * d], preferred_element_type=F32)
    v_scr[...] = jnp.dot(hb, w_ref[:, 2 * d:3 * d], preferred_element_type=F32)
    if fresh:
        for slot in range(kt_ref.shape[0]):
            if slot == layer:
                kt_ref[slot] = k_ref[...].T
                vt_ref[slot] = v_scr[...].T
            else:
                kt_ref[slot] = jnp.zeros(kt_ref.shape[1:], F32)
                vt_ref[slot] = jnp.zeros(vt_ref.shape[1:], F32)
    else:
        kt_ref[...] = k_ref[...].T
        vt_ref[...] = v_scr[...].T


def _norm_qkv_t(x2d, g, w, seq_len, layer, n_layers, buffers):
    n = x2d.shape[0]
    tm = min(ROW_TILE, seq_len)
    assert seq_len % tm == 0 and tm % LANES == 0
    tps = seq_len // tm
    fresh = buffers[0] is None
    assert fresh == (buffers[1] is None)
    row_spec = pl.BlockSpec((tm, D_MODEL), lambda i: (i, 0))
    if fresh:
        t_spec = pl.BlockSpec((n_layers, None, D_MODEL, tm), lambda i: (0, i // tps, 0, i % tps))
    else:
        t_spec = pl.BlockSpec((None, None, D_MODEL, tm), lambda i: (layer, i // tps, 0, i % tps))
    t_shape = jax.ShapeDtypeStruct((n_layers, n // seq_len, D_MODEL, seq_len), F32)
    in_specs = [row_spec, _const_spec((1, D_MODEL)), _const_spec(w.shape)]
    args = [x2d, g.reshape(1, D_MODEL), w]
    aliases = {}
    if not fresh:
        for k, buf in enumerate(buffers):
            aliases[len(args)] = 2 + k
            in_specs.append(pl.BlockSpec(memory_space=pl.ANY))
            args.append(buf)
    return pl.pallas_call(
        functools.partial(_norm_qkv_t_kernel, layer=layer, fresh=fresh),
        grid=(n // tm,),
        in_specs=in_specs,
        out_specs=[row_spec, row_spec, t_spec, t_spec],
        out_shape=[jax.ShapeDtypeStruct((n, D_MODEL), F32)] * 2 + [t_shape] * 2,
        scratch_shapes=[pltpu.VMEM((tm, D_MODEL), F32)],
        input_output_aliases=aliases,
        compiler_params=_params(("parallel",)),
        name="norm_qkv_t",
    )(*args)


def _post_kernel(*refs, per_seq_prev, tiles_per_seq, final_norm):
    if per_seq_prev:
        (x_ref, o_ref, wo_ref, g_ref, win_ref, cw_ref, cb_ref, wout_ref, gf_ref, p1_ref, p2_ref,
         y_ref, u_ref) = refs
    else:
        (x_ref, o_ref, wo_ref, g_ref, win_ref, cw_ref, cb_ref, wout_ref, gf_ref,
         y_ref, tail_ref, carry_scr) = refs

        @pl.when(pl.program_id(0) % tiles_per_seq == 0)
        def _():
            carry_scr[...] = jnp.zeros_like(carry_scr)

    tm = x_ref.shape[0]
    x = x_ref[...] + jnp.dot(o_ref[...].astype(BF16), wo_ref[...], preferred_element_type=F32)
    hb = _rms(x, g_ref[...]).astype(BF16)
    if per_seq_prev:
        t_in_seq = lax.broadcasted_iota(jnp.int32, (tm, 1), 0) % SUBLANES
    acc = x
    for c in range(D_FF // FFN_COL_CHUNK):
        conv = []
        for off in (c * FFN_COL_CHUNK, D_FF + c * FFN_COL_CHUNK):
            cols = slice(off, off + FFN_COL_CHUNK)
            u = jnp.dot(hb, win_ref[:, cols], preferred_element_type=F32)
            if per_seq_prev:
                u1 = jnp.where(t_in_seq < 1, p1_ref[:, cols], pltpu.roll(u, 1, 0))
                u2 = jnp.where(t_in_seq < 2, p2_ref[:, cols], pltpu.roll(u, 2, 0))
                u_ref[:, cols] = u
            else:
                tail = carry_scr[:, cols]
                u1 = _shift_rows(u, tail, 1)
                u2 = _shift_rows(u, tail, 2)
                carry_scr[:, cols] = u[tm - SUBLANES:tm]
                tail_ref[:, cols] = u[tm - SUBLANES:tm]
            conv.append(u2 * cw_ref[0:1, cols] + u1 * cw_ref[1:2, cols] + u * cw_ref[2:3, cols]
                        + cb_ref[:, cols])
        act = (_silu(conv[0]) * conv[1]).astype(BF16)
        acc = acc + jnp.dot(act, wout_ref[c * FFN_COL_CHUNK:(c + 1) * FFN_COL_CHUNK, :],
                            preferred_element_type=F32)
    y_ref[...] = _rms(acc, gf_ref[...]) if final_norm else acc


def _post(x2d, o2d, w_o, g, w_in, conv_w, conv_b, w_out, g_final, final_norm, seq_len, prev=None):
    n = x2d.shape[0]
    tm = min(ROW_TILE, n)
    nseq = n // seq_len
    row_spec = pl.BlockSpec((tm, D_MODEL), lambda i: (i, 0))
    in_specs = [row_spec, row_spec,
                _const_spec((D_MODEL, D_MODEL)), _const_spec((1, D_MODEL)),
                _const_spec((D_MODEL, 2 * D_FF)), _const_spec((FFN_CONV, 2 * D_FF)),
                _const_spec((1, 2 * D_FF)), _const_spec((D_FF, D_MODEL)), _const_spec((1, D_MODEL))]
    args = [x2d, o2d, w_o, g.reshape(1, D_MODEL), w_in, conv_w, conv_b.reshape(1, 2 * D_FF), w_out,
            g_final.reshape(1, D_MODEL)]
    y_shape = jax.ShapeDtypeStruct((n, D_MODEL), F32)
    if prev is None:
        assert seq_len % tm == 0
        tps = seq_len // tm
        y, tail = pl.pallas_call(
            functools.partial(_post_kernel, per_seq_prev=False, tiles_per_seq=tps, final_norm=final_norm),
            grid=(n // tm,),
            in_specs=in_specs,
            out_specs=[row_spec, pl.BlockSpec((None, SUBLANES, 2 * D_FF), lambda i: (i // tps, 0, 0))],
            out_shape=[y_shape, jax.ShapeDtypeStruct((nseq, SUBLANES, 2 * D_FF), F32)],
            scratch_shapes=[pltpu.VMEM((SUBLANES, 2 * D_FF), F32)],
            compiler_params=_params(("arbitrary",)),
            name="post_ffn_prompt",
        )(*args)
        return y, tail[:, SUBLANES - (FFN_CONV - 1):]
    assert seq_len == SUBLANES and n == tm
    pad = ((0, 0), (0, SUBLANES - 1), (0, 0))
    p1 = jnp.pad(prev[:, 1:2], pad).reshape(n, 2 * D_FF)
    p2 = jnp.pad(prev, ((0, 0), (0, SUBLANES - 2), (0, 0))).reshape(n, 2 * D_FF)
    wide_spec = pl.BlockSpec((tm, 2 * D_FF), lambda i: (i, 0))
    y, u = pl.pallas_call(
        functools.partial(_post_kernel, per_seq_prev=True, tiles_per_seq=1, final_norm=final_norm),
        grid=(1,),
        in_specs=in_specs + [wide_spec, wide_spec],
        out_specs=[row_spec, wide_spec],
        out_shape=[y_shape, jax.ShapeDtypeStruct((n, 2 * D_FF), F32)],
        compiler_params=_params(("arbitrary",)),
        name="post_ffn_sample",
    )(*args, p1, p2)
    return y, u.reshape(nseq, seq_len, 2 * D_FF)[:, seq_len - (FFN_CONV - 1):]


def _topk_bias_t(gate_t, n_valid, n_rows):
    row = lax.broadcasted_iota(jnp.int32, gate_t.shape, 0)
    valid = row < n_valid
    out = []
    for n in range(n_rows):
        gn = gate_t[n:n + 1, :]
        beats = jnp.where(gate_t > gn, 1.0, jnp.where((gate_t == gn) & (row < n), 1.0, 0.0))
        rank = jnp.sum(jnp.where(valid, beats, 0.0), axis=0, keepdims=True)
        out.append(jnp.where(rank < MOBA_TOPK, 0.0, NEG_INF))
    return out


def _moba_prompt_tile(c, q_ref, o_ref, kb_scr, vt_scr, km_scr):
    blk = MOBA_BLOCK
    hd = A_HEAD_DIM
    q = q_ref[...]
    lane = lax.broadcasted_iota(jnp.int32, (1, LANES), 1)
    causal = (lax.broadcasted_iota(jnp.int32, (blk, blk), 0)
              <= lax.broadcasted_iota(jnp.int32, (blk, blk), 1))
    pair = range(2)
    qh = [jnp.where(lane // hd == hh, q, 0.0) for hh in pair]
    qs = [(x * (hd ** -0.5 * math.log2(math.e))).astype(BF16) for x in qh]
    s = [[lax.dot_general(kb_scr[n], qs[hh], NT_DIMS, preferred_element_type=F32) for n in range(c + 1)]
         for hh in pair]
    for hh in pair:
        s[hh][c] = jnp.where(causal, s[hh][c], NEG_INF)
        if c > MOBA_TOPK:
            gate_t = _dot3(km_scr[...], qh[hh], NT_DIMS)
            for n, b in enumerate(_topk_bias_t(gate_t, c, c)):
                s[hh][n] = s[hh][n] + b
    m = []
    for hh in pair:
        mv = s[hh][c]
        for n in range(c):
            mv = jnp.maximum(mv, s[hh][n])
        m.append(jnp.max(mv, axis=0, keepdims=True))
    o_t = []
    for hh in pair:
        lv = None
        acc = None
        for n in range(c + 1):
            p = jnp.exp2(s[hh][n] - m[hh])
            lv = p if lv is None else lv + p
            pv = jnp.dot(vt_scr[n, hh * hd:(hh + 1) * hd, :], p.astype(BF16),
                         preferred_element_type=F32)
            acc = pv if acc is None else acc + pv
        o_t.append(acc / jnp.sum(lv, axis=0, keepdims=True))
    o_ref[...] = jnp.concatenate(o_t, axis=0).T.astype(o_ref.dtype)


def _moba_prompt_kernel(q_ref, k_ref, vt_ref, o_ref, kb_scr, vt_scr, km_scr, *, nb):
    i = pl.program_id(2)
    blk = MOBA_BLOCK

    @pl.when(i == 0)
    def _():
        km_scr[...] = jnp.zeros_like(km_scr)
        for n in range(nb):
            kn = k_ref[n * blk:(n + 1) * blk, :]
            kb_scr[n] = kn.astype(BF16)
            vt_scr[n] = vt_ref[:, n * blk:(n + 1) * blk].astype(BF16)
            km_scr[n:n + 1, :] = jnp.mean(kn, axis=0, keepdims=True)

    for c in range(nb):
        pl.when(i == c)(functools.partial(_moba_prompt_tile, c, q_ref, o_ref, kb_scr, vt_scr, km_scr))


def _moba_prompt_attn(q, k, v_t, layer):
    bsz, s, _ = q.shape
    assert s % MOBA_BLOCK == 0
    nb = s // MOBA_BLOCK
    nb_pad = -(-nb // (2 * SUBLANES)) * 2 * SUBLANES
    hp = D_MODEL // LANES
    q_spec = pl.BlockSpec((None, MOBA_BLOCK, LANES), lambda b, h, i: (b, i, h))
    k_spec = pl.BlockSpec((None, s, LANES), lambda b, h, i: (b, 0, h))
    vt_spec = pl.BlockSpec((None, None, LANES, s), lambda b, h, i: (layer, b, h, 0))
    return pl.pallas_call(
        functools.partial(_moba_prompt_kernel, nb=nb),
        grid=(bsz, hp, nb),
        in_specs=[q_spec, k_spec, vt_spec],
        out_specs=q_spec,
        out_shape=jax.ShapeDtypeStruct((bsz, s, D_MODEL), BF16),
        scratch_shapes=[pltpu.VMEM((nb, MOBA_BLOCK, LANES), BF16),
                        pltpu.VMEM((nb, LANES, MOBA_BLOCK), BF16),
                        pltpu.VMEM((nb_pad, LANES), F32)],
        compiler_params=_params(("parallel", "parallel", "arbitrary")),
        name="moba_prompt_attn",
    )(q, k, v_t)


def _topk_select(gate, n_valid, n_cols):
    col = lax.broadcasted_iota(jnp.int32, gate.shape, 1)
    valid = col < n_valid
    sel = []
    for n in range(n_cols):
        gn = gate[:, n:n + 1]
        beats = jnp.where(gate > gn, 1.0, jnp.where((gate == gn) & (col < n), 1.0, 0.0))
        rank = jnp.sum(jnp.where(valid, beats, 0.0), axis=1, keepdims=True)
        sel.append(rank < MOBA_TOPK)
    return sel


def _moba_sample_kernel(pt_ref, q_ref, kn_ref, vn_ref, *refs, n_pages, t_new):
    g_pages = SAMPLE_PAGES_PER_STEP
    k_refs = refs[:g_pages]
    v_refs = refs[g_pages:2 * g_pages]
    o_ref = refs[2 * g_pages]
    qbf_scr, new_scr, gate_scr, sc_scr, l_scr, acc_scr = refs[2 * g_pages + 1:]
    del pt_ref
    j = pl.program_id(1)
    n_steps = n_pages // g_pages
    n_blocks = n_pages * PAGE_SIZE // MOBA_BLOCK
    pages_per_block = MOBA_BLOCK // PAGE_SIZE
    blocks_per_step = g_pages // pages_per_block
    rows = A_HEADS * t_new
    head_of_row = lax.broadcasted_iota(jnp.int32, (rows, D_MODEL), 0) // t_new
    head_of_lane = lax.broadcasted_iota(jnp.int32, (rows, D_MODEL), 1) // A_HEAD_DIM

    @pl.when(j == 0)
    def _():
        qbd = jnp.where(head_of_row == head_of_lane,
                        jnp.concatenate([q_ref[...]] * A_HEADS, axis=0), 0.0)
        qbf_scr[...] = (qbd * (A_HEAD_DIM ** -0.5)).astype(BF16)
        gate_scr[...] = jnp.zeros_like(gate_scr)

    @pl.when(j < n_steps)
    def _():
        col = lax.broadcasted_iota(jnp.int32, gate_scr.shape, 1)
        gate = gate_scr[...]
        for bb in range(blocks_per_step):
            ssum = None
            for g in range(bb * pages_per_block, (bb + 1) * pages_per_block):
                s = jnp.dot(qbf_scr[...], k_refs[g][...].astype(BF16), preferred_element_type=F32)
                sc_scr[j * g_pages + g] = s
                ssum = s if ssum is None else ssum + s
            gmean = jnp.sum(ssum, axis=1, keepdims=True) * (1.0 / MOBA_BLOCK)
            gate = jnp.where(col == j * blocks_per_step + bb, gmean, gate)
        gate_scr[...] = gate

    @pl.when(j == n_steps - 1)
    def _():
        sel = _topk_select(gate_scr[...], n_blocks, n_blocks)
        new_scr[...] = jnp.zeros_like(new_scr)
        new_scr[0, 0:t_new, :] = kn_ref[...].astype(BF16)
        new_scr[1, 0:t_new, :] = vn_ref[...].astype(BF16)
        s_own = lax.dot_general(qbf_scr[...], new_scr[0], NT_DIMS, preferred_element_type=F32)
        r_tok = lax.broadcasted_iota(jnp.int32, s_own.shape, 0) % t_new
        c_tok = lax.broadcasted_iota(jnp.int32, s_own.shape, 1)
        s_own = jnp.where(c_tok <= r_tok, s_own, NEG_INF)
        mv = s_own
        for b in range(n_blocks):
            bias = jnp.broadcast_to(jnp.where(sel[b], 0.0, NEG_INF), s_own.shape)
            for p in range(b * pages_per_block, (b + 1) * pages_per_block):
                mv = jnp.maximum(mv, sc_scr[p] + bias)
        m = jnp.max(mv, axis=1, keepdims=True)
        p_own = jnp.exp(s_own - m)
        lv = p_own
        for b in range(n_blocks):
            shift = jnp.broadcast_to(jnp.where(sel[b], 0.0, NEG_INF) - m, s_own.shape)
            for p in range(b * pages_per_block, (b + 1) * pages_per_block):
                e = jnp.exp(sc_scr[p] + shift)
                lv = lv + e
                sc_scr[p] = e
        l_scr[...] = jnp.sum(lv, axis=1, keepdims=True)
        acc_scr[...] = jnp.dot(p_own.astype(BF16), new_scr[1], preferred_element_type=F32)

    @pl.when(j >= n_steps)
    def _():
        acc = acc_scr[...]
        for g in range(g_pages):
            pp = sc_scr[(j - n_steps) * g_pages + g].astype(BF16)
            acc = acc + lax.dot_general(pp, v_refs[g][...].astype(BF16), NT_DIMS,
                                        preferred_element_type=F32)
        acc_scr[...] = acc

    @pl.when(j == 2 * n_steps - 1)
    def _():
        acc = jnp.where(head_of_row == head_of_lane, acc_scr[...] / l_scr[...], 0.0)
        o = acc[0:t_new]
        for h in range(1, A_HEADS):
            o = o + acc[h * t_new:(h + 1) * t_new]
        o_ref[...] = o


def _moba_sample_attn(q, k_new, v_new, cache_kt, cache_vt, page_table, layer):
    bsz, t_new, _ = q.shape
    n_pages = page_table.shape[1]
    g_pages = SAMPLE_PAGES_PER_STEP
    assert (n_pages * PAGE_SIZE) % MOBA_BLOCK == 0 and MOBA_BLOCK % PAGE_SIZE == 0
    assert (g_pages * PAGE_SIZE) % MOBA_BLOCK == 0 and n_pages % g_pages == 0
    assert t_new == SUBLANES and A_HEADS * t_new == LANES and PAGE_SIZE == LANES
    n_steps = n_pages // g_pages
    n_blocks = n_pages * PAGE_SIZE // MOBA_BLOCK
    assert MOBA_TOPK <= n_blocks <= LANES
    tok_spec = pl.BlockSpec((None, t_new, D_MODEL), lambda b, j, pt: (b, 0, 0))
    new_spec = pl.BlockSpec((None, None, t_new, D_MODEL), lambda b, j, pt: (layer, b, 0, 0))

    def k_spec(g):
        return pl.BlockSpec(
            (None, None, D_MODEL, PAGE_SIZE),
            lambda b, j, pt: (layer, pt[b, g_pages * jnp.minimum(j, n_steps - 1) + g], 0, 0))

    def v_spec(g):
        def index(b, j, pt):
            in_v = j >= n_steps
            seq = jnp.where(in_v, b, jnp.maximum(b - 1, 0))
            step = jnp.where(in_v, j - n_steps, n_steps - 1)
            return (layer, pt[seq, g_pages * step + g], 0, 0)
        return pl.BlockSpec((None, None, D_MODEL, PAGE_SIZE), index)

    rows = A_HEADS * t_new
    grid_spec = pltpu.PrefetchScalarGridSpec(
        num_scalar_prefetch=1,
        grid=(bsz, 2 * n_steps),
        in_specs=[tok_spec, new_spec, new_spec] + [k_spec(g) for g in range(g_pages)]
        + [v_spec(g) for g in range(g_pages)],
        out_specs=tok_spec,
        scratch_shapes=[pltpu.VMEM((rows, D_MODEL), BF16),
                        pltpu.VMEM((2, LANES, D_MODEL), BF16), pltpu.VMEM((rows, LANES), F32),
                        pltpu.VMEM((n_pages, rows, PAGE_SIZE), F32), pltpu.VMEM((rows, 1), F32),
                        pltpu.VMEM((rows, D_MODEL), F32)],
    )
    return pl.pallas_call(
        functools.partial(_moba_sample_kernel, n_pages=n_pages, t_new=t_new),
        grid_spec=grid_spec,
        out_shape=jax.ShapeDtypeStruct((bsz, t_new, D_MODEL), F32),
        compiler_params=_params(("parallel", "arbitrary")),
        name="moba_sample_attn",
    )(page_table, q, k_new, v_new, *([cache_kt] * g_pages), *([cache_vt] * g_pages))


def _gla_kernel(q_ref, k_ref, v_ref, r_ref, glr_ref, wg_ref, bg_ref, ng_ref, s0_ref,
                o_ref, sout_ref, s_scr, *, valid_rows):
    ci = pl.program_id(1)
    c = B_CHUNK
    n_sub = q_ref.shape[0] // c

    @pl.when(ci == 0)
    def _():
        s_scr[...] = s0_ref[...]

    causal = (lax.broadcasted_iota(jnp.int32, (c, c), 1) <= lax.broadcasted_iota(jnp.int32, (c, c), 0))
    st = [s_scr[h] for h in range(B_HEADS)]
    for j in range(n_sub):
        rows = slice(j * c, (j + 1) * c)
        x = (jnp.dot(glr_ref[rows, :].astype(BF16), wg_ref[...], preferred_element_type=F32)
             + bg_ref[...])
        gk = -_softplus(-x) * (1.0 / B_GATE_TAU)
        if valid_rows < c:
            gk = jnp.where(lax.broadcasted_iota(jnp.int32, gk.shape, 0) < valid_rows, gk, 0.0)
        b = _cumsum_rows(gk)
        b_last = b[c - 1:c, :]
        k = k_ref[rows, :]
        q_t = (q_ref[rows, :] * (B_DK ** -0.5) * jnp.exp(b)).astype(BF16)
        k_t = (k * jnp.exp(-b)).astype(BF16)
        k_end = (k * jnp.exp(b_last - b)).astype(BF16)
        e_last = jnp.exp(b_last)
        for h in range(B_HEADS):
            kc = slice(h * B_DK, (h + 1) * B_DK)
            vc = slice(h * B_DV, (h + 1) * B_DV)
            vh = v_ref[rows, vc].astype(BF16)
            att = jnp.where(causal, lax.dot_general(q_t[:, kc], k_t[:, kc], NT_DIMS,
                                                    preferred_element_type=F32), 0.0)
            o = (jnp.dot(att.astype(BF16), vh, preferred_element_type=F32)
                 + lax.dot_general(q_t[:, kc], st[h].astype(BF16), NT_DIMS,
                                   preferred_element_type=F32))
            st[h] = st[h] * e_last[:, kc] + lax.dot_general(vh, k_end[:, kc], TN_DIMS,
                                                            preferred_element_type=F32)
            o = o * lax.rsqrt(jnp.mean(o * o, axis=-1, keepdims=True) + NORM_EPS) * ng_ref[...]
            o_ref[rows, vc] = (o * _silu(r_ref[rows, vc])).astype(o_ref.dtype)
    for h in range(B_HEADS):
        s_scr[h] = st[h]

    @pl.when(ci == pl.num_programs(1) - 1)
    def _():
        sout_ref[...] = s_scr[...]


def _gla_scan(q, k, v, r, glr, w_gate, b_gate, norm_g, s0_t, valid_rows):
    bsz, t, _ = q.shape
    assert t % B_CHUNK == 0
    rows = math.gcd(t, GLA_CHUNKS_PER_STEP * B_CHUNK)
    assert valid_rows == B_CHUNK or rows == B_CHUNK
    nk = B_HEADS * B_DK

    def tok(width):
        return pl.BlockSpec((None, rows, width), lambda b, ci: (b, ci, 0))

    st_spec = pl.BlockSpec((None, B_HEADS, B_DV, B_DK), lambda b, ci: (b, 0, 0, 0))
    return pl.pallas_call(
        functools.partial(_gla_kernel, valid_rows=valid_rows),
        grid=(bsz, t // rows),
        in_specs=[tok(nk), tok(nk), tok(D_MODEL), tok(D_MODEL), tok(LANES),
                  _const_spec((LANES, nk)), _const_spec((1, nk)), _const_spec((1, B_DV)), st_spec],
        out_specs=[tok(D_MODEL), st_spec],
        out_shape=[jax.ShapeDtypeStruct((bsz, t, D_MODEL), BF16),
                   jax.ShapeDtypeStruct((bsz, B_HEADS, B_DV, B_DK), F32)],
        scratch_shapes=[pltpu.VMEM((B_HEADS, B_DV, B_DK), F32)],
        compiler_params=_params(("parallel", "arbitrary")),
        name="gla_scan",
    )(q, k, v, r, glr, w_gate, b_gate.reshape(1, nk), norm_g.reshape(1, B_DV), s0_t)


def _split_bf16(x):
    hi = x.astype(BF16)
    return hi, (x - hi.astype(F32)).astype(BF16)


def _dot3(a, b, dims=(((1,), (0,)), ((), ()))):
    ah, al = _split_bf16(a)
    bh, bl = _split_bf16(b)

    def f(x, y):
        return lax.dot_general(x, y, dims, preferred_element_type=F32)
    return f(ah, bh) + (f(ah, bl) + f(al, bh))


def _unit_lower_inverse_minus_eye(a_list):
    c = a_list[0].shape[0]
    base = SUBLANES
    assert c % base == 0 and (c // base) & (c // base - 1) == 0
    ri = lax.broadcasted_iota(jnp.int32, (c, c), 0)
    cj = lax.broadcasted_iota(jnp.int32, (c, c), 1)
    x = [jnp.where(ri // base == cj // base, -a, 0.0) for a in a_list]
    n = list(x)
    p = 2
    while p < base:
        x = [_dot3(xi, xi) for xi in x]
        n = [ni + xi + _dot3(ni, xi) for ni, xi in zip(n, x)]
        p *= 2
    m = base
    while m < c:
        pair = (ri // (2 * m) == cj // (2 * m)) & (ri // m != cj // m)
        low = [jnp.where(pair, a, 0.0) for a in a_list]
        d_l = [li + _dot3(ni, li) for ni, li in zip(n, low)]
        n = [ni - dli - _dot3(dli, ni) for ni, dli in zip(n, d_l)]
        m *= 2
    return n


def _gdn_kernel(qkv_ref, z_ref, ba_ref, cw_ref, alog_ref, dtb_ref, ng_ref, s0_ref,
                o_ref, sout_ref, s_scr, carry_scr, *, valid_lo, valid_hi):
    ci = pl.program_id(1)
    c = qkv_ref.shape[0]
    heads = range(C_HEADS)

    @pl.when(ci == 0)
    def _():
        s_scr[...] = s0_ref[...]
        carry_scr[0:SUBLANES, :] = jnp.zeros((SUBLANES, carry_scr.shape[1]), F32)

    u = qkv_ref[...]
    carry_scr[SUBLANES:SUBLANES + c, :] = u
    conv = u * cw_ref[C_CONV - 1:C_CONV, :]
    for s in range(1, C_CONV):
        conv = conv + carry_scr[SUBLANES - s:SUBLANES - s + c, :] * cw_ref[C_CONV - 1 - s:C_CONV - s, :]
    carry_scr[0:SUBLANES, :] = u[c - SUBLANES:c]
    xc = _silu(conv)

    ba = ba_ref[...]
    beta_all = _sigmoid(ba)
    g_all = -jnp.exp(alog_ref[...]) * _softplus(ba + dtb_ref[...])
    if valid_lo > 0 or valid_hi < c:
        rowi = lax.broadcasted_iota(jnp.int32, ba.shape, 0)
        ok = (rowi >= valid_lo) & (rowi < valid_hi)
        beta_all = jnp.where(ok, beta_all, 0.0)
        g_all = jnp.where(ok, g_all, 0.0)
    gam_all = _cumsum_rows(g_all)
    gam_t = jnp.concatenate([gam_all, jnp.zeros((LANES - c, LANES), F32)], axis=0).T
    ri = lax.broadcasted_iota(jnp.int32, (c, c), 0)
    cj = lax.broadcasted_iota(jnp.int32, (c, c), 1)
    incl = cj <= ri
    strict = cj < ri

    qn, kn, gcol, dec, rhs, a_mat = [], [], [], [], [], []
    for h in heads:
        qh = xc[:, h * C_DK:(h + 1) * C_DK]
        kh = xc[:, D_MODEL + h * C_DK:D_MODEL + (h + 1) * C_DK]
        vh = xc[:, 2 * D_MODEL + h * C_DV:2 * D_MODEL + (h + 1) * C_DV]
        qn.append(qh * lax.rsqrt(jnp.sum(qh * qh, axis=-1, keepdims=True) + NORM_EPS) * (C_DK ** -0.5))
        kn.append(kh * lax.rsqrt(jnp.sum(kh * kh, axis=-1, keepdims=True) + NORM_EPS))
        beta = beta_all[:, h:h + 1]
        gcol.append(gam_all[:, C_HEADS + h:C_HEADS + h + 1])
        grow = gam_t[C_HEADS + h:C_HEADS + h + 1, 0:c]
        dec.append(jnp.where(incl, jnp.exp(jnp.where(incl, gcol[h] - grow, 0.0)), 0.0))
        kb = kn[h] * beta
        rhs.append(jnp.concatenate([kb * jnp.exp(gcol[h]), vh * beta], axis=1))
        a_mat.append(jnp.where(strict, _dot3(kb, kn[h], NT_DIMS) * dec[h], 0.0))
    inv_m = _unit_lower_inverse_minus_eye(a_mat)
    wu = [rhs[h] + _dot3(inv_m[h], rhs[h]) for h in heads]
    att = [_bdot_nt(qn[h], kn[h]) * dec[h] for h in heads]

    st = [s_scr[h] for h in heads]
    stb = [s.astype(BF16) for s in st]
    v_new = [wu[h][:, C_DK:] - lax.dot_general(wu[h][:, :C_DK].astype(BF16), stb[h], NT_DIMS,
                                               preferred_element_type=F32) for h in heads]
    o = [lax.dot_general((qn[h] * jnp.exp(gcol[h])).astype(BF16), stb[h], NT_DIMS,
                         preferred_element_type=F32) + _bdot(att[h], v_new[h]) for h in heads]
    for h in heads:
        g_last = gcol[h][c - 1:c, :]
        s_scr[h] = jnp.exp(g_last) * st[h] + _bdot_tn(v_new[h], kn[h] * jnp.exp(g_last - gcol[h]))
    for h in heads:
        hc = slice(h * C_DV, (h + 1) * C_DV)
        oh = o[h] * lax.rsqrt(jnp.mean(o[h] * o[h], axis=-1, keepdims=True) + NORM_EPS) * ng_ref[...]
        o_ref[:, hc] = (oh * _silu(z_ref[:, hc])).astype(o_ref.dtype)

    @pl.when(ci == pl.num_programs(1) - 1)
    def _():
        sout_ref[...] = s_scr[...]


def _gdn_scan(qkv, z, ba, conv_w, a_log, dt_bias, norm_g, s0_t, valid_lo, valid_hi):
    bsz, t, _ = qkv.shape
    c = C_CHUNK
    assert t % c == 0

    def tok(width):
        return pl.BlockSpec((None, c, width), lambda b, ci: (b, ci, 0))

    def under_a(p):
        return jnp.zeros((1, LANES), F32).at[0, C_HEADS:2 * C_HEADS].set(p)

    st_spec = pl.BlockSpec((None, C_HEADS, C_DV, C_DK), lambda b, ci: (b, 0, 0, 0))
    return pl.pallas_call(
        functools.partial(_gdn_kernel, valid_lo=valid_lo, valid_hi=valid_hi),
        grid=(bsz, t // c),
        in_specs=[tok(3 * D_MODEL), tok(D_MODEL), tok(LANES),
                  _const_spec((C_CONV, 3 * D_MODEL)), _const_spec((1, LANES)),
                  _const_spec((1, LANES)), _const_spec((1, C_DV)), st_spec],
        out_specs=[tok(D_MODEL), st_spec],
        out_shape=[jax.ShapeDtypeStruct((bsz, t, D_MODEL), BF16),
                   jax.ShapeDtypeStruct((bsz, C_HEADS, C_DV, C_DK), F32)],
        scratch_shapes=[pltpu.VMEM((C_HEADS, C_DV, C_DK), F32),
                        pltpu.VMEM((SUBLANES + c, 3 * D_MODEL), F32)],
        compiler_params=_params(("parallel", "arbitrary")),
        name="gdn_scan",
    )(qkv, z, ba, conv_w, under_a(a_log), under_a(dt_bias), norm_g.reshape(1, C_DV), s0_t)


def _pad_cols(w, width):
    return jnp.pad(w, ((0, 0), (0, width - w.shape[1])))


def _pad_tokens(a, before, total):
    return jnp.pad(a, ((0, 0), (before, total - before - a.shape[1]), (0, 0)))


def _moba_mixer(x, g, w_in, layer, n_layers, kv_rows, cache=None):
    bsz, t, _ = x.shape
    x2d = x.reshape(bsz * t, D_MODEL)
    if cache is None:
        q, k, k_t, v_t = _norm_qkv_t(x2d, g, w_in.astype(BF16), t, layer, n_layers, kv_rows)
        o = _moba_prompt_attn(q.reshape(bsz, t, D_MODEL), k.reshape(bsz, t, D_MODEL), v_t, layer)
        return o.reshape(bsz * t, D_MODEL), (k_t, v_t)
    splits = tuple((i * D_MODEL, (i + 1) * D_MODEL) for i in range(3))
    if kv_rows[0] is None:
        kv_rows = tuple(jnp.zeros((n_layers, bsz * t, D_MODEL), F32) for _ in range(2))
    q, k, v = _norm_proj(x2d, g, w_in.astype(BF16), splits, stack=(layer, n_layers, kv_rows))
    k4 = k.reshape(n_layers, bsz, t, D_MODEL)
    v4 = v.reshape(n_layers, bsz, t, D_MODEL)
    o = _moba_sample_attn(q.reshape(bsz, t, D_MODEL), k4, v4, *cache, layer)
    return o.reshape(bsz * t, D_MODEL), (k, v)


def _gla_mixer(x, g, w_in, w_gate, b_gate, norm_g, s0):
    bsz, t, _ = x.shape
    nk = B_HEADS * B_DK
    edges = (0, nk, 2 * nk, 2 * nk + D_MODEL, 2 * nk + 2 * D_MODEL, 2 * nk + 2 * D_MODEL + LANES)
    splits = tuple(zip(edges[:-1], edges[1:]))
    parts = _norm_proj(x.reshape(bsz * t, D_MODEL), g, _pad_cols(w_in, edges[-1]).astype(BF16), splits)
    t_pad = -(-t // B_CHUNK) * B_CHUNK
    q, k, v, r, glr = (_pad_tokens(a.reshape(bsz, t, -1), 0, t_pad) for a in parts)
    wg = jnp.pad(w_gate, ((0, LANES - B_GATE_RANK), (0, 0))).astype(BF16)
    if s0 is None:
        s0_t = jnp.zeros((bsz, B_HEADS, B_DV, B_DK), F32)
    else:
        s0_t = jnp.swapaxes(s0, -1, -2)
    o, s_t = _gla_scan(q, k, v, r, glr, wg, b_gate, norm_g, s0_t, min(t, B_CHUNK))
    return o[:, :t].reshape(bsz * t, D_MODEL), jnp.swapaxes(s_t, -1, -2)


def _gdn_mixer(x, g, w_in, conv_w, a_log, dt_bias, norm_g, s0, conv_prev):
    bsz, t, _ = x.shape
    edges = (0, 3 * D_MODEL, 4 * D_MODEL, 4 * D_MODEL + LANES)
    splits = tuple(zip(edges[:-1], edges[1:]))
    parts = _norm_proj(x.reshape(bsz * t, D_MODEL), g, _pad_cols(w_in, edges[-1]).astype(BF16), splits)
    qkv, z, ba = (a.reshape(bsz, t, -1) for a in parts)
    hist = C_CONV - 1
    if conv_prev is None:
        lo = 0
        conv_new = qkv[:, t - hist:]
        s0_t = jnp.zeros((bsz, C_HEADS, C_DV, C_DK), F32)
        qkv_in = qkv
    else:
        lo = hist
        assert t >= hist
        conv_new = qkv[:, t - hist:]
        s0_t = jnp.swapaxes(s0, -1, -2)
        qkv_in = jnp.concatenate([conv_prev, qkv], axis=1)
    t_pad = -(-(lo + t) // C_CHUNK) * C_CHUNK
    if t_pad != lo + t or lo:
        assert t_pad == C_CHUNK
    qkv_in = _pad_tokens(qkv_in, 0, t_pad)
    z = _pad_tokens(z, lo, t_pad)
    ba = _pad_tokens(ba, lo, t_pad)
    o, s_t = _gdn_scan(qkv_in, z, ba, conv_w, a_log, dt_bias, norm_g, s0_t, lo, min(lo + t, C_CHUNK))
    return o[:, lo:lo + t].reshape(bsz * t, D_MODEL), jnp.swapaxes(s_t, -1, -2), conv_new


def _run_group(x, cache, state_gla, state_gdn, state_gdn_conv, state_ffn_conv, w):
    bsz, t, _ = x.shape
    x2d = x.reshape(bsz * t, D_MODEL)
    out = {name: [] for name in ("gla", "gdn", "gconv", "fconv")}
    n_moba = w["moba_w_in"].shape[0]
    kv_rows = (None, None)
    for i in range(DEPTH):
        j = i // N_MIXERS
        xin = x2d.reshape(bsz, t, D_MODEL)
        if i % N_MIXERS == 0:
            o, kv_rows = _moba_mixer(xin, w["norm_mix"][i], w["moba_w_in"][j], j, n_moba, kv_rows, cache)
            w_o = w["moba_w_out"][j]
        elif i % N_MIXERS == 1:
            s0 = None if state_gla is None else state_gla[j]
            o, s = _gla_mixer(xin, w["norm_mix"][i], w["gla_w_in"][j], w["gla_w_gate"][j],
                              w["gla_b_gate"][j], w["gla_norm"][j], s0)
            out["gla"].append(s)
            w_o = w["gla_w_out"][j]
        else:
            s0 = None if state_gdn is None else state_gdn[j]
            cp = None if state_gdn_conv is None else state_gdn_conv[j]
            o, s, cn = _gdn_mixer(xin, w["norm_mix"][i], w["gdn_w_in"][j], w["gdn_conv_w"][j],
                                  w["gdn_a_log"][j], w["gdn_dt_bias"][j], w["gdn_norm"][j], s0, cp)
            out["gdn"].append(s)
            out["gconv"].append(cn)
            w_o = w["gdn_w_out"][j]
        prev = None if state_ffn_conv is None else state_ffn_conv[i]
        x2d, fc = _post(x2d, o, w_o.astype(BF16), w["norm_ffn"][i], w["ffn_w_in"][i].astype(BF16),
                        w["ffn_conv_w"][i], w["ffn_conv_b"][i], w["ffn_w_out"][i].astype(BF16),
                        w["norm_final"], i == DEPTH - 1, t, prev)
        out["fconv"].append(fc)
    y = x2d.reshape(bsz, t, D_MODEL)
    res = {name: jnp.stack(v) for name, v in out.items()}
    if cache is None:
        res["k"], res["v"] = (a.reshape(n_moba, bsz, A_HEADS, A_HEAD_DIM, t).transpose(0, 1, 4, 2, 3)
                              for a in kv_rows)
    else:
        rows = (n_moba, bsz, t, A_HEADS, A_HEAD_DIM)
        res["k"], res["v"] = kv_rows[0].reshape(rows), kv_rows[1].reshape(rows)
    return y, res


def kernel(x_prompt, x_sample, cache_k, cache_v, page_table, state_gla, state_gdn, state_gdn_conv,
           state_ffn_conv, norm_mix, norm_ffn, norm_final, moba_w_in, moba_w_out, gla_w_in, gla_w_gate,
           gla_b_gate, gla_norm, gla_w_out, gdn_w_in, gdn_conv_w, gdn_a_log, gdn_dt_bias, gdn_norm,
           gdn_w_out, ffn_w_in, ffn_conv_w, ffn_conv_b, ffn_w_out):
    w = dict(norm_mix=norm_mix, norm_ffn=norm_ffn, norm_final=norm_final, moba_w_in=moba_w_in,
             moba_w_out=moba_w_out, gla_w_in=gla_w_in, gla_w_gate=gla_w_gate, gla_b_gate=gla_b_gate,
             gla_norm=gla_norm, gla_w_out=gla_w_out, gdn_w_in=gdn_w_in, gdn_conv_w=gdn_conv_w,
             gdn_a_log=gdn_a_log, gdn_dt_bias=gdn_dt_bias, gdn_norm=gdn_norm, gdn_w_out=gdn_w_out,
             ffn_w_in=ffn_w_in, ffn_conv_w=ffn_conv_w, ffn_conv_b=ffn_conv_b, ffn_w_out=ffn_w_out)
    n_layers, n_pool = cache_k.shape[:2]
    pool_shape = (n_layers, n_pool, D_MODEL, PAGE_SIZE)
    cache = (cache_k.transpose(0, 1, 3, 4, 2).reshape(pool_shape),
             cache_v.transpose(0, 1, 3, 4, 2).reshape(pool_shape), page_table)
    yp, p = _run_group(x_prompt, None, None, None, None, None, w)
    ys, s = _run_group(x_sample, cache, state_gla, state_gdn, state_gdn_conv, state_ffn_conv, w)
    return (yp, ys, p["k"], p["v"], s["k"], s["v"], p["gla"], s["gla"], p["gdn"], s["gdn"],
            p["gconv"], s["gconv"], p["fconv"], s["fconv"])
```

```python
import functools
import math

import jax
import jax.numpy as jnp
from jax import lax
from jax.experimental import pallas as pl
from jax.experimental.pallas import tpu as pltpu

F32 = jnp.float32
BF16 = jnp.bfloat16

D_MODEL = 1024
DEPTH = 4
N_MIXERS = 3
NORM_EPS = 1e-6
NEG_INF = -1e30
PAGE_SIZE = 128

A_HEADS = 16
A_HEAD_DIM = D_MODEL // A_HEADS
MOBA_BLOCK = 256
MOBA_TOPK = 3

B_HEADS = 4
B_DK = D_MODEL // 2 // B_HEADS
B_DV = D_MODEL // B_HEADS
B_GATE_RANK = 16
B_GATE_TAU = 16.0
B_CHUNK = 64

C_HEADS = 8
C_DK = D_MODEL // C_HEADS
C_DV = D_MODEL // C_HEADS
C_CONV = 4
C_CHUNK = 64

D_FF = 2816
FFN_CONV = 3

LANES = 128
SUBLANES = 8
ROW_TILE = 512
FFN_COL_CHUNK = 2816
GLA_CHUNKS_PER_STEP = 8
SAMPLE_PAGES_PER_STEP = 16
VMEM_LIMIT = 56 * 1024 * 1024

NT_DIMS = (((1,), (1,)), ((), ()))
TN_DIMS = (((0,), (0,)), ((), ()))


def _params(sem, vmem=VMEM_LIMIT):
    return pltpu.CompilerParams(dimension_semantics=sem, vmem_limit_bytes=vmem)


def _const_spec(shape):
    nd = len(shape)
    return pl.BlockSpec(shape, lambda *_: (0,) * nd, pipeline_mode=pl.Buffered(1))


def _rms(x, g):
    return x * lax.rsqrt(jnp.mean(x * x, axis=-1, keepdims=True) + NORM_EPS) * g


def _sigmoid(x):
    return 1.0 / (1.0 + jnp.exp(-x))


def _silu(x):
    return x * _sigmoid(x)


def _softplus(x):
    return jnp.maximum(x, 0.0) + jnp.log(1.0 + jnp.exp(-jnp.abs(x)))


def _bdot(a, b):
    return jnp.dot(a.astype(BF16), b.astype(BF16), preferred_element_type=F32)


def _bdot_nt(a, b):
    return lax.dot_general(a.astype(BF16), b.astype(BF16), NT_DIMS, preferred_element_type=F32)


def _bdot_tn(a, b):
    return lax.dot_general(a.astype(BF16), b.astype(BF16), TN_DIMS, preferred_element_type=F32)


def _cumsum_rows(x):
    n = x.shape[0]
    row = lax.broadcasted_iota(jnp.int32, x.shape, 0)
    s = 1
    while s < n:
        x = x + jnp.where(row >= s, pltpu.roll(x, s, 0), 0.0)
        s *= 2
    return x


def _shift_rows(u, tail, s):
    r = pltpu.roll(u, s, 0)
    row = lax.broadcasted_iota(jnp.int32, tail.shape, 0)
    head = jnp.where(row < s, pltpu.roll(tail, s, 0), r[0:SUBLANES])
    return jnp.concatenate([head, r[SUBLANES:]], axis=0)


def _norm_proj_kernel(x_ref, g_ref, w_ref, *refs, splits):
    out_refs = refs[len(refs) - len(splits):]
    hb = _rms(x_ref[...], g_ref[...]).astype(BF16)
    for o_ref, (c0, c1) in zip(out_refs, splits):
        o_ref[...] = jnp.dot(hb, w_ref[:, c0:c1], preferred_element_type=F32).astype(o_ref.dtype)


def _norm_proj(x2d, g, w, splits, stack=None):
    n = x2d.shape[0]
    tm = min(ROW_TILE, n)
    out_shape = [jax.ShapeDtypeStruct((n, c1 - c0), F32) for c0, c1 in splits]
    out_specs = [pl.BlockSpec((tm, c1 - c0), lambda i: (i, 0)) for c0, c1 in splits]
    in_specs = [pl.BlockSpec((tm, D_MODEL), lambda i: (i, 0)), _const_spec((1, D_MODEL)),
                _const_spec(w.shape)]
    args = [x2d, g.reshape(1, D_MODEL), w]
    aliases = {}
    if stack is not None:
        layer, n_layers, buffers = stack
        first = len(splits) - len(buffers)
        for k, buf in enumerate(buffers):
            c0, c1 = splits[first + k]
            out_shape[first + k] = jax.ShapeDtypeStruct((n_layers, n, c1 - c0), F32)
            out_specs[first + k] = pl.BlockSpec((None, tm, c1 - c0), lambda i: (layer, i, 0))
            if buf is not None:
                aliases[len(args)] = first + k
                in_specs.append(pl.BlockSpec(memory_space=pl.ANY))
                args.append(buf)
    return pl.pallas_call(
        functools.partial(_norm_proj_kernel, splits=splits),
        grid=(n // tm,),
        in_specs=in_specs,
        out_specs=out_specs,
        out_shape=out_shape,
        input_output_aliases=aliases,
        compiler_params=_params(("parallel",)),
        name="norm_proj",
    )(*args)


def _norm_qkv_t_kernel(x_ref, g_ref, w_ref, *refs, layer, fresh):
    q_ref, k_ref, kt_ref, vt_ref, v_scr = refs[len(refs) - 5:]
    d = D_MODEL
    hb = _rms(x_ref[...], g_ref[...]).astype(BF16)
    q_ref[...] = jnp.dot(hb, w_ref[:, 0:d], preferred_element_type=F32)
    k_ref[...] = jnp.dot(hb, w_ref[:, d:2 * d], preferred_element_type=F32)
    v_scr[...] = jnp.dot(hb, w_ref[:, 2 * d:3 * d], preferred_element_type=F32)
    if fresh:
        for slot in range(kt_ref.shape[0]):
            if slot == layer:
                kt_ref[slot] = k_ref[...].T
                vt_ref[slot] = v_scr[...].T
            else:
                kt_ref[slot] = jnp.zeros(kt_ref.shape[1:], F32)
                vt_ref[slot] = jnp.zeros(vt_ref.shape[1:], F32)
    else:
        kt_ref[...] = k_ref[...].T
        vt_ref[...] = v_scr[...].T


def _norm_qkv_t(x2d, g, w, seq_len, layer, n_layers, buffers):
    n = x2d.shape[0]
    tm = min(ROW_TILE, seq_len)
    assert seq_len % tm == 0 and tm % LANES == 0
    tps = seq_len // tm
    fresh = buffers[0] is None
    assert fresh == (buffers[1] is None)
    row_spec = pl.BlockSpec((tm, D_MODEL), lambda i: (i, 0))
    if fresh:
        t_spec = pl.BlockSpec((n_layers, None, D_MODEL, tm), lambda i: (0, i // tps, 0, i % tps))
    else:
        t_spec = pl.BlockSpec((None, None, D_MODEL, tm), lambda i: (layer, i // tps, 0, i % tps))
    t_shape = jax.ShapeDtypeStruct((n_layers, n // seq_len, D_MODEL, seq_len), F32)
    in_specs = [row_spec, _const_spec((1, D_MODEL)), _const_spec(w.shape)]
    args = [x2d, g.reshape(1, D_MODEL), w]
    aliases = {}
    if not fresh:
        for k, buf in enumerate(buffers):
            aliases[len(args)] = 2 + k
            in_specs.append(pl.BlockSpec(memory_space=pl.ANY))
            args.append(buf)
    return pl.pallas_call(
        functools.partial(_norm_qkv_t_kernel, layer=layer, fresh=fresh),
        grid=(n // tm,),
        in_specs=in_specs,
        out_specs=[row_spec, row_spec, t_spec, t_spec],
        out_shape=[jax.ShapeDtypeStruct((n, D_MODEL), F32)] * 2 + [t_shape] * 2,
        scratch_shapes=[pltpu.VMEM((tm, D_MODEL), F32)],
        input_output_aliases=aliases,
        compiler_params=_params(("parallel",)),
        name="norm_qkv_t",
    )(*args)


def _post_kernel(*refs, per_seq_prev, tiles_per_seq, final_norm):
    if per_seq_prev:
        (x_ref, o_ref, wo_ref, g_ref, win_ref, cw_ref, cb_ref, wout_ref, gf_ref, p1_ref, p2_ref,
         y_ref, u_ref) = refs
    else:
        (x_ref, o_ref, wo_ref, g_ref, win_ref, cw_ref, cb_ref, wout_ref, gf_ref,
         y_ref, tail_ref, carry_scr) = refs

        @pl.when(pl.program_id(0) % tiles_per_seq == 0)
        def _():
            carry_scr[...] = jnp.zeros_like(carry_scr)

    tm = x_ref.shape[0]
    x = x_ref[...] + jnp.dot(o_ref[...].astype(BF16), wo_ref[...], preferred_element_type=F32)
    hb = _rms(x, g_ref[...]).astype(BF16)
    if per_seq_prev:
        t_in_seq = lax.broadcasted_iota(jnp.int32, (tm, 1), 0) % SUBLANES
    acc = x
    for c in range(D_FF // FFN_COL_CHUNK):
        conv = []
        for off in (c * FFN_COL_CHUNK, D_FF + c * FFN_COL_CHUNK):
            cols = slice(off, off + FFN_COL_CHUNK)
            u = jnp.dot(hb, win_ref[:, cols], preferred_element_type=F32)
            if per_seq_prev:
                u1 = jnp.where(t_in_seq < 1, p1_ref[:, cols], pltpu.roll(u, 1, 0))
                u2 = jnp.where(t_in_seq < 2, p2_ref[:, cols], pltpu.roll(u, 2, 0))
                u_ref[:, cols] = u
            else:
                tail = carry_scr[:, cols]
                u1 = _shift_rows(u, tail, 1)
                u2 = _shift_rows(u, tail, 2)
                carry_scr[:, cols] = u[tm - SUBLANES:tm]
                tail_ref[:, cols] = u[tm - SUBLANES:tm]
            conv.append(u2 * cw_ref[0:1, cols] + u1 * cw_ref[1:2, cols] + u * cw_ref[2:3, cols]
                        + cb_ref[:, cols])
        act = (_silu(conv[0]) * conv[1]).astype(BF16)
        acc = acc + jnp.dot(act, wout_ref[c * FFN_COL_CHUNK:(c + 1) * FFN_COL_CHUNK, :],
                            preferred_element_type=F32)
    y_ref[...] = _rms(acc, gf_ref[...]) if final_norm else acc


def _post(x2d, o2d, w_o, g, w_in, conv_w, conv_b, w_out, g_final, final_norm, seq_len, prev=None):
    n = x2d.shape[0]
    tm = min(ROW_TILE, n)
    nseq = n // seq_len
    row_spec = pl.BlockSpec((tm, D_MODEL), lambda i: (i, 0))
    in_specs = [row_spec, row_spec,
                _const_spec((D_MODEL, D_MODEL)), _const_spec((1, D_MODEL)),
                _const_spec((D_MODEL, 2 * D_FF)), _const_spec((FFN_CONV, 2 * D_FF)),
                _const_spec((1, 2 * D_FF)), _const_spec((D_FF, D_MODEL)), _const_spec((1, D_MODEL))]
    args = [x2d, o2d, w_o, g.reshape(1, D_MODEL), w_in, conv_w, conv_b.reshape(1, 2 * D_FF), w_out,
            g_final.reshape(1, D_MODEL)]
    y_shape = jax.ShapeDtypeStruct((n, D_MODEL), F32)
    if prev is None:
        assert seq_len % tm == 0
        tps = seq_len // tm
        y, tail = pl.pallas_call(
            functools.partial(_post_kernel, per_seq_prev=False, tiles_per_seq=tps, final_norm=final_norm),
            grid=(n // tm,),
            in_specs=in_specs,
            out_specs=[row_spec, pl.BlockSpec((None, SUBLANES, 2 * D_FF), lambda i: (i // tps, 0, 0))],
            out_shape=[y_shape, jax.ShapeDtypeStruct((nseq, SUBLANES, 2 * D_FF), F32)],
            scratch_shapes=[pltpu.VMEM((SUBLANES, 2 * D_FF), F32)],
            compiler_params=_params(("arbitrary",)),
            name="post_ffn_prompt",
        )(*args)
        return y, tail[:, SUBLANES - (FFN_CONV - 1):]
    assert seq_len == SUBLANES and n == tm
    pad = ((0, 0), (0, SUBLANES - 1), (0, 0))
    p1 = jnp.pad(prev[:, 1:2], pad).reshape(n, 2 * D_FF)
    p2 = jnp.pad(prev, ((0, 0), (0, SUBLANES - 2), (0, 0))).reshape(n, 2 * D_FF)
    wide_spec = pl.BlockSpec((tm, 2 * D_FF), lambda i: (i, 0))
    y, u = pl.pallas_call(
        functools.partial(_post_kernel, per_seq_prev=True, tiles_per_seq=1, final_norm=final_norm),
        grid=(1,),
        in_specs=in_specs + [wide_spec, wide_spec],
        out_specs=[row_spec, wide_spec],
        out_shape=[y_shape, jax.ShapeDtypeStruct((n, 2 * D_FF), F32)],
        compiler_params=_params(("arbitrary",)),
        name="post_ffn_sample",
    )(*args, p1, p2)
    return y, u.reshape(nseq, seq_len, 2 * D_FF)[:, seq_len - (FFN_CONV - 1):]


def _topk_bias_t(gate_t, n_valid, n_rows):
    row = lax.broadcasted_iota(jnp.int32, gate_t.shape, 0)
    valid = row < n_valid
    out = []
    for n in range(n_rows):
        gn = gate_t[n:n + 1, :]
        beats = jnp.where(gate_t > gn, 1.0, jnp.where((gate_t == gn) & (row < n), 1.0, 0.0))
        rank = jnp.sum(jnp.where(valid, beats, 0.0), axis=0, keepdims=True)
        out.append(jnp.where(rank < MOBA_TOPK, 0.0, NEG_INF))
    return out


def _moba_prompt_tile(c, q_ref, o_ref, kb_scr, vt_scr, km_scr):
    blk = MOBA_BLOCK
    hd = A_HEAD_DIM
    q = q_ref[...]
    lane = lax.broadcasted_iota(jnp.int32, (1, LANES), 1)
    causal = (lax.broadcasted_iota(jnp.int32, (blk, blk), 0)
              <= lax.broadcasted_iota(jnp.int32, (blk, blk), 1))
    pair = range(2)
    qh = [jnp.where(lane // hd == hh, q, 0.0) for hh in pair]
    qs = [(x * (hd ** -0.5 * math.log2(math.e))).astype(BF16) for x in qh]
    s = [[lax.dot_general(kb_scr[n], qs[hh], NT_DIMS, preferred_element_type=F32) for n in range(c + 1)]
         for hh in pair]
    for hh in pair:
        s[hh][c] = jnp.where(causal, s[hh][c], NEG_INF)
        if c > MOBA_TOPK:
            gate_t = _dot3(km_scr[...], qh[hh], NT_DIMS)
            for n, b in enumerate(_topk_bias_t(gate_t, c, c)):
                s[hh][n] = s[hh][n] + b
    m = []
    for hh in pair:
        mv = s[hh][c]
        for n in range(c):
            mv = jnp.maximum(mv, s[hh][n])
        m.append(jnp.max(mv, axis=0, keepdims=True))
    o_t = []
    for hh in pair:
        lv = None
        acc = None
        for n in range(c + 1):
            p = jnp.exp2(s[hh][n] - m[hh])
            lv = p if lv is None else lv + p
            pv = jnp.dot(vt_scr[n, hh * hd:(hh + 1) * hd, :], p.astype(BF16),
                         preferred_element_type=F32)
            acc = pv if acc is None else acc + pv
        o_t.append(acc / jnp.sum(lv, axis=0, keepdims=True))
    o_ref[...] = jnp.concatenate(o_t, axis=0).T.astype(o_ref.dtype)


def _moba_prompt_kernel(q_ref, k_ref, vt_ref, o_ref, kb_scr, vt_scr, km_scr, *, nb):
    i = pl.program_id(2)
    blk = MOBA_BLOCK

    @pl.when(i == 0)
    def _():
        km_scr[...] = jnp.zeros_like(km_scr)
        for n in range(nb):
            kn = k_ref[n * blk:(n + 1) * blk, :]
            kb_scr[n] = kn.astype(BF16)
            vt_scr[n] = vt_ref[:, n * blk:(n + 1) * blk].astype(BF16)
            km_scr[n:n + 1, :] = jnp.mean(kn, axis=0, keepdims=True)

    for c in range(nb):
        pl.when(i == c)(functools.partial(_moba_prompt_tile, c, q_ref, o_ref, kb_scr, vt_scr, km_scr))


def _moba_prompt_attn(q, k, v_t, layer):
    bsz, s, _ = q.shape
    assert s % MOBA_BLOCK == 0
    nb = s // MOBA_BLOCK
    nb_pad = -(-nb // (2 * SUBLANES)) * 2 * SUBLANES
    hp = D_MODEL // LANES
    q_spec = pl.BlockSpec((None, MOBA_BLOCK, LANES), lambda b, h, i: (b, i, h))
    k_spec = pl.BlockSpec((None, s, LANES), lambda b, h, i: (b, 0, h))
    vt_spec = pl.BlockSpec((None, None, LANES, s), lambda b, h, i: (layer, b, h, 0))
    return pl.pallas_call(
        functools.partial(_moba_prompt_kernel, nb=nb),
        grid=(bsz, hp, nb),
        in_specs=[q_spec, k_spec, vt_spec],
        out_specs=q_spec,
        out_shape=jax.ShapeDtypeStruct((bsz, s, D_MODEL), BF16),
        scratch_shapes=[pltpu.VMEM((nb, MOBA_BLOCK, LANES), BF16),
                        pltpu.VMEM((nb, LANES, MOBA_BLOCK), BF16),
                        pltpu.VMEM((nb_pad, LANES), F32)],
        compiler_params=_params(("parallel", "parallel", "arbitrary")),
        name="moba_prompt_attn",
    )(q, k, v_t)


def _topk_select(gate, n_valid, n_cols):
    col = lax.broadcasted_iota(jnp.int32, gate.shape, 1)
    valid = col < n_valid
    sel = []
    for n in range(n_cols):
        gn = gate[:, n:n + 1]
        beats = jnp.where(gate > gn, 1.0, jnp.where((gate == gn) & (col < n), 1.0, 0.0))
        rank = jnp.sum(jnp.where(valid, beats, 0.0), axis=1, keepdims=True)
        sel.append(rank < MOBA_TOPK)
    return sel


def _moba_sample_kernel(pt_ref, q_ref, kn_ref, vn_ref, *refs, n_pages, t_new):
    g_pages = SAMPLE_PAGES_PER_STEP
    k_refs = refs[:g_pages]
    v_refs = refs[g_pages:2 * g_pages]
    o_ref = refs[2 * g_pages]
    qbf_scr, new_scr, gate_scr, sc_scr, l_scr, acc_scr = refs[2 * g_pages + 1:]
    del pt_ref
    j = pl.program_id(1)
    n_steps = n_pages // g_pages
    n_blocks = n_pages * PAGE_SIZE // MOBA_BLOCK
    pages_per_block = MOBA_BLOCK // PAGE_SIZE
    blocks_per_step = g_pages // pages_per_block
    rows = A_HEADS * t_new
    head_of_row = lax.broadcasted_iota(jnp.int32, (rows, D_MODEL), 0) // t_new
    head_of_lane = lax.broadcasted_iota(jnp.int32, (rows, D_MODEL), 1) // A_HEAD_DIM

    @pl.when(j == 0)
    def _():
        qbd = jnp.where(head_of_row == head_of_lane,
                        jnp.concatenate([q_ref[...]] * A_HEADS, axis=0), 0.0)
        qbf_scr[...] = (qbd * (A_HEAD_DIM ** -0.5)).astype(BF16)
        gate_scr[...] = jnp.zeros_like(gate_scr)

    @pl.when(j < n_steps)
    def _():
        col = lax.broadcasted_iota(jnp.int32, gate_scr.shape, 1)
        gate = gate_scr[...]
        for bb in range(blocks_per_step):
            ssum = None
            for g in range(bb * pages_per_block, (bb + 1) * pages_per_block):
                s = jnp.dot(qbf_scr[...], k_refs[g][...].astype(BF16), preferred_element_type=F32)
                sc_scr[j * g_pages + g] = s
                ssum = s if ssum is None else ssum + s
            gmean = jnp.sum(ssum, axis=1, keepdims=True) * (1.0 / MOBA_BLOCK)
            gate = jnp.where(col == j * blocks_per_step + bb, gmean, gate)
        gate_scr[...] = gate

    @pl.when(j == n_steps - 1)
    def _():
        sel = _topk_select(gate_scr[...], n_blocks, n_blocks)
        new_scr[...] = jnp.zeros_like(new_scr)
        new_scr[0, 0:t_new, :] = kn_ref[...].astype(BF16)
        new_scr[1, 0:t_new, :] = vn_ref[...].astype(BF16)
        s_own = lax.dot_general(qbf_scr[...], new_scr[0], NT_DIMS, preferred_element_type=F32)
        r_tok = lax.broadcasted_iota(jnp.int32, s_own.shape, 0) % t_new
        c_tok = lax.broadcasted_iota(jnp.int32, s_own.shape, 1)
        s_own = jnp.where(c_tok <= r_tok, s_own, NEG_INF)
        mv = s_own
        for b in range(n_blocks):
            bias = jnp.broadcast_to(jnp.where(sel[b], 0.0, NEG_INF), s_own.shape)
            for p in range(b * pages_per_block, (b + 1) * pages_per_block):
                mv = jnp.maximum(mv, sc_scr[p] + bias)
        m = jnp.max(mv, axis=1, keepdims=True)
        p_own = jnp.exp(s_own - m)
        lv = p_own
        for b in range(n_blocks):
            shift = jnp.broadcast_to(jnp.where(sel[b], 0.0, NEG_INF) - m, s_own.shape)
            for p in range(b * pages_per_block, (b + 1) * pages_per_block):
                e = jnp.exp(sc_scr[p] + shift)
                lv = lv + e
                sc_scr[p] = e
        l_scr[...] = jnp.sum(lv, axis=1, keepdims=True)
        acc_scr[...] = jnp.dot(p_own.astype(BF16), new_scr[1], preferred_element_type=F32)

    @pl.when(j >= n_steps)
    def _():
        acc = acc_scr[...]
        for g in range(g_pages):
            pp = sc_scr[(j - n_steps) * g_pages + g].astype(BF16)
            acc = acc + lax.dot_general(pp, v_refs[g][...].astype(BF16), NT_DIMS,
                                        preferred_element_type=F32)
        acc_scr[...] = acc

    @pl.when(j == 2 * n_steps - 1)
    def _():
        acc = jnp.where(head_of_row == head_of_lane, acc_scr[...] / l_scr[...], 0.0)
        o = acc[0:t_new]
        for h in range(1, A_HEADS):
            o = o + acc[h * t_new:(h + 1) * t_new]
        o_ref[...] = o


def _moba_sample_attn(q, k_new, v_new, cache_kt, cache_vt, page_table, layer):
    bsz, t_new, _ = q.shape
    n_pages = page_table.shape[1]
    g_pages = SAMPLE_PAGES_PER_STEP
    assert (n_pages * PAGE_SIZE) % MOBA_BLOCK == 0 and MOBA_BLOCK % PAGE_SIZE == 0
    assert (g_pages * PAGE_SIZE) % MOBA_BLOCK == 0 and n_pages % g_pages == 0
    assert t_new == SUBLANES and A_HEADS * t_new == LANES and PAGE_SIZE == LANES
    n_steps = n_pages // g_pages
    n_blocks = n_pages * PAGE_SIZE // MOBA_BLOCK
    assert MOBA_TOPK <= n_blocks <= LANES
    tok_spec = pl.BlockSpec((None, t_new, D_MODEL), lambda b, j, pt: (b, 0, 0))
    new_spec = pl.BlockSpec((None, None, t_new, D_MODEL), lambda b, j, pt: (layer, b, 0, 0))

    def k_spec(g):
        return pl.BlockSpec(
            (None, None, D_MODEL, PAGE_SIZE),
            lambda b, j, pt: (layer, pt[b, g_pages * jnp.minimum(j, n_steps - 1) + g], 0, 0))

    def v_spec(g):
        def index(b, j, pt):
            in_v = j >= n_steps
            seq = jnp.where(in_v, b, jnp.maximum(b - 1, 0))
            step = jnp.where(in_v, j - n_steps, n_steps - 1)
            return (layer, pt[seq, g_pages * step + g], 0, 0)
        return pl.BlockSpec((None, None, D_MODEL, PAGE_SIZE), index)

    rows = A_HEADS * t_new
    grid_spec = pltpu.PrefetchScalarGridSpec(
        num_scalar_prefetch=1,
        grid=(bsz, 2 * n_steps),
        in_specs=[tok_spec, new_spec, new_spec] + [k_spec(g) for g in range(g_pages)]
        + [v_spec(g) for g in range(g_pages)],
        out_specs=tok_spec,
        scratch_shapes=[pltpu.VMEM((rows, D_MODEL), BF16),
                        pltpu.VMEM((2, LANES, D_MODEL), BF16), pltpu.VMEM((rows, LANES), F32),
                        pltpu.VMEM((n_pages, rows, PAGE_SIZE), F32), pltpu.VMEM((rows, 1), F32),
                        pltpu.VMEM((rows, D_MODEL), F32)],
    )
    return pl.pallas_call(
        functools.partial(_moba_sample_kernel, n_pages=n_pages, t_new=t_new),
        grid_spec=grid_spec,
        out_shape=jax.ShapeDtypeStruct((bsz, t_new, D_MODEL), F32),
        compiler_params=_params(("parallel", "arbitrary")),
        name="moba_sample_attn",
    )(page_table, q, k_new, v_new, *([cache_kt] * g_pages), *([cache_vt] * g_pages))


def _gla_kernel(q_ref, k_ref, v_ref, r_ref, glr_ref, wg_ref, bg_ref, ng_ref, s0_ref,
                o_ref, sout_ref, s_scr, *, valid_rows):
    ci = pl.program_id(1)
    c = B_CHUNK
    n_sub = q_ref.shape[0] // c

    @pl.when(ci == 0)
    def _():
        s_scr[...] = s0_ref[...]

    causal = (lax.broadcasted_iota(jnp.int32, (c, c), 1) <= lax.broadcasted_iota(jnp.int32, (c, c), 0))
    st = [s_scr[h] for h in range(B_HEADS)]
    for j in range(n_sub):
        rows = slice(j * c, (j + 1) * c)
        x = (jnp.dot(glr_ref[rows, :].astype(BF16), wg_ref[...], preferred_element_type=F32)
             + bg_ref[...])
        gk = -_softplus(-x) * (1.0 / B_GATE_TAU)
        if valid_rows < c:
            gk = jnp.where(lax.broadcasted_iota(jnp.int32, gk.shape, 0) < valid_rows, gk, 0.0)
        b = _cumsum_rows(gk)
        b_last = b[c - 1:c, :]
        k = k_ref[rows, :]
        q_t = (q_ref[rows, :] * (B_DK ** -0.5) * jnp.exp(b)).astype(BF16)
        k_t = (k * jnp.exp(-b)).astype(BF16)
        k_end = (k * jnp.exp(b_last - b)).astype(BF16)
        e_last = jnp.exp(b_last)
        for h in range(B_HEADS):
            kc = slice(h * B_DK, (h + 1) * B_DK)
            vc = slice(h * B_DV, (h + 1) * B_DV)
            vh = v_ref[rows, vc].astype(BF16)
            att = jnp.where(causal, lax.dot_general(q_t[:, kc], k_t[:, kc], NT_DIMS,
                                                    preferred_element_type=F32), 0.0)
            o = (jnp.dot(att.astype(BF16), vh, preferred_element_type=F32)
                 + lax.dot_general(q_t[:, kc], st[h].astype(BF16), NT_DIMS,
                                   preferred_element_type=F32))
            st[h] = st[h] * e_last[:, kc] + lax.dot_general(vh, k_end[:, kc], TN_DIMS,
                                                            preferred_element_type=F32)
            o = o * lax.rsqrt(jnp.mean(o * o, axis=-1, keepdims=True) + NORM_EPS) * ng_ref[...]
            o_ref[rows, vc] = (o * _silu(r_ref[rows, vc])).astype(o_ref.dtype)
    for h in range(B_HEADS):
        s_scr[h] = st[h]

    @pl.when(ci == pl.num_programs(1) - 1)
    def _():
        sout_ref[...] = s_scr[...]


def _gla_scan(q, k, v, r, glr, w_gate, b_gate, norm_g, s0_t, valid_rows):
    bsz, t, _ = q.shape
    assert t % B_CHUNK == 0
    rows = math.gcd(t, GLA_CHUNKS_PER_STEP * B_CHUNK)
    assert valid_rows == B_CHUNK or rows == B_CHUNK
    nk = B_HEADS * B_DK

    def tok(width):
        return pl.BlockSpec((None, rows, width), lambda b, ci: (b, ci, 0))

    st_spec = pl.BlockSpec((None, B_HEADS, B_DV, B_DK), lambda b, ci: (b, 0, 0, 0))
    return pl.pallas_call(
        functools.partial(_gla_kernel, valid_rows=valid_rows),
        grid=(bsz, t // rows),
        in_specs=[tok(nk), tok(nk), tok(D_MODEL), tok(D_MODEL), tok(LANES),
                  _const_spec((LANES, nk)), _const_spec((1, nk)), _const_spec((1, B_DV)), st_spec],
        out_specs=[tok(D_MODEL), st_spec],
        out_shape=[jax.ShapeDtypeStruct((bsz, t, D_MODEL), BF16),
                   jax.ShapeDtypeStruct((bsz, B_HEADS, B_DV, B_DK), F32)],
        scratch_shapes=[pltpu.VMEM((B_HEADS, B_DV, B_DK), F32)],
        compiler_params=_params(("parallel", "arbitrary")),
        name="gla_scan",
    )(q, k, v, r, glr, w_gate, b_gate.reshape(1, nk), norm_g.reshape(1, B_DV), s0_t)


def _split_bf16(x):
    hi = x.astype(BF16)
    return hi, (x - hi.astype(F32)).astype(BF16)


def _dot3(a, b, dims=(((1,), (0,)), ((), ()))):
    ah, al = _split_bf16(a)
    bh, bl = _split_bf16(b)

    def f(x, y):
        return lax.dot_general(x, y, dims, preferred_element_type=F32)
    return f(ah, bh) + (f(ah, bl) + f(al, bh))


def _unit_lower_inverse_minus_eye(a_list):
    c = a_list[0].shape[0]
    base = SUBLANES
    assert c % base == 0 and (c // base) & (c // base - 1) == 0
    ri = lax.broadcasted_iota(jnp.int32, (c, c), 0)
    cj = lax.broadcasted_iota(jnp.int32, (c, c), 1)
    x = [jnp.where(ri // base == cj // base, -a, 0.0) for a in a_list]
    n = list(x)
    p = 2
    while p < base:
        x = [_dot3(xi, xi) for xi in x]
        n = [ni + xi + _dot3(ni, xi) for ni, xi in zip(n, x)]
        p *= 2
    m = base
    while m < c:
        pair = (ri // (2 * m) == cj // (2 * m)) & (ri // m != cj // m)
        low = [jnp.where(pair, a, 0.0) for a in a_list]
        d_l = [li + _dot3(ni, li) for ni, li in zip(n, low)]
        n = [ni - dli - _dot3(dli, ni) for ni, dli in zip(n, d_l)]
        m *= 2
    return n


def _gdn_kernel(qkv_ref, z_ref, ba_ref, cw_ref, alog_ref, dtb_ref, ng_ref, s0_ref,
                o_ref, sout_ref, s_scr, carry_scr, *, valid_lo, valid_hi):
    ci = pl.program_id(1)
    c = qkv_ref.shape[0]
    heads = range(C_HEADS)

    @pl.when(ci == 0)
    def _():
        s_scr[...] = s0_ref[...]
        carry_scr[0:SUBLANES, :] = jnp.zeros((SUBLANES, carry_scr.shape[1]), F32)

    u = qkv_ref[...]
    carry_scr[SUBLANES:SUBLANES + c, :] = u
    conv = u * cw_ref[C_CONV - 1:C_CONV, :]
    for s in range(1, C_CONV):
        conv = conv + carry_scr[SUBLANES - s:SUBLANES - s + c, :] * cw_ref[C_CONV - 1 - s:C_CONV - s, :]
    carry_scr[0:SUBLANES, :] = u[c - SUBLANES:c]
    xc = _silu(conv)

    ba = ba_ref[...]
    beta_all = _sigmoid(ba)
    g_all = -jnp.exp(alog_ref[...]) * _softplus(ba + dtb_ref[...])
    if valid_lo > 0 or valid_hi < c:
        rowi = lax.broadcasted_iota(jnp.int32, ba.shape, 0)
        ok = (rowi >= valid_lo) & (rowi < valid_hi)
        beta_all = jnp.where(ok, beta_all, 0.0)
        g_all = jnp.where(ok, g_all, 0.0)
    gam_all = _cumsum_rows(g_all)
    gam_t = jnp.concatenate([gam_all, jnp.zeros((LANES - c, LANES), F32)], axis=0).T
    ri = lax.broadcasted_iota(jnp.int32, (c, c), 0)
    cj = lax.broadcasted_iota(jnp.int32, (c, c), 1)
    incl = cj <= ri
    strict = cj < ri

    qn, kn, gcol, dec, rhs, a_mat = [], [], [], [], [], []
    for h in heads:
        qh = xc[:, h * C_DK:(h + 1) * C_DK]
        kh = xc[:, D_MODEL + h * C_DK:D_MODEL + (h + 1) * C_DK]
        vh = xc[:, 2 * D_MODEL + h * C_DV:2 * D_MODEL + (h + 1) * C_DV]
        qn.append(qh * lax.rsqrt(jnp.sum(qh * qh, axis=-1, keepdims=True) + NORM_EPS) * (C_DK ** -0.5))
        kn.append(kh * lax.rsqrt(jnp.sum(kh * kh, axis=-1, keepdims=True) + NORM_EPS))
        beta = beta_all[:, h:h + 1]
        gcol.append(gam_all[:, C_HEADS + h:C_HEADS + h + 1])
        grow = gam_t[C_HEADS + h:C_HEADS + h + 1, 0:c]
        dec.append(jnp.where(incl, jnp.exp(jnp.where(incl, gcol[h] - grow, 0.0)), 0.0))
        kb = kn[h] * beta
        rhs.append(jnp.concatenate([kb * jnp.exp(gcol[h]), vh * beta], axis=1))
        a_mat.append(jnp.where(strict, _dot3(kb, kn[h], NT_DIMS) * dec[h], 0.0))
    inv_m = _unit_lower_inverse_minus_eye(a_mat)
    wu = [rhs[h] + _dot3(inv_m[h], rhs[h]) for h in heads]
    att = [_bdot_nt(qn[h], kn[h]) * dec[h] for h in heads]

    st = [s_scr[h] for h in heads]
    stb = [s.astype(BF16) for s in st]
    v_new = [wu[h][:, C_DK:] - lax.dot_general(wu[h][:, :C_DK].astype(BF16), stb[h], NT_DIMS,
                                               preferred_element_type=F32) for h in heads]
    o = [lax.dot_general((qn[h] * jnp.exp(gcol[h])).astype(BF16), stb[h], NT_DIMS,
                         preferred_element_type=F32) + _bdot(att[h], v_new[h]) for h in heads]
    for h in heads:
        g_last = gcol[h][c - 1:c, :]
        s_scr[h] = jnp.exp(g_last) * st[h] + _bdot_tn(v_new[h], kn[h] * jnp.exp(g_last - gcol[h]))
    for h in heads:
        hc = slice(h * C_DV, (h + 1) * C_DV)
        oh = o[h] * lax.rsqrt(jnp.mean(o[h] * o[h], axis=-1, keepdims=True) + NORM_EPS) * ng_ref[...]
        o_ref[:, hc] = (oh * _silu(z_ref[:, hc])).astype(o_ref.dtype)

    @pl.when(ci == pl.num_programs(1) - 1)
    def _():
        sout_ref[...] = s_scr[...]


def _gdn_scan(qkv, z, ba, conv_w, a_log, dt_bias, norm_g, s0_t, c, valid_lo, valid_hi):
    bsz, t, _ = qkv.shape
    assert t % c == 0

    def tok(width):
        return pl.BlockSpec((None, c, width), lambda b, ci: (b, ci, 0))

    def under_a(p):
        return jnp.zeros((1, LANES), F32).at[0, C_HEADS:2 * C_HEADS].set(p)

    st_spec = pl.BlockSpec((None, C_HEADS, C_DV, C_DK), lambda b, ci: (b, 0, 0, 0))
    return pl.pallas_call(
        functools.partial(_gdn_kernel, valid_lo=valid_lo, valid_hi=valid_hi),
        grid=(bsz, t // c),
        in_specs=[tok(3 * D_MODEL), tok(D_MODEL), tok(LANES),
                  _const_spec((C_CONV, 3 * D_MODEL)), _const_spec((1, LANES)),
                  _const_spec((1, LANES)), _const_spec((1, C_DV)), st_spec],
        out_specs=[tok(D_MODEL), st_spec],
        out_shape=[jax.ShapeDtypeStruct((bsz, t, D_MODEL), BF16),
                   jax.ShapeDtypeStruct((bsz, C_HEADS, C_DV, C_DK), F32)],
        scratch_shapes=[pltpu.VMEM((C_HEADS, C_DV, C_DK), F32),
                        pltpu.VMEM((SUBLANES + c, 3 * D_MODEL), F32)],
        compiler_params=_params(("parallel", "arbitrary")),
        name="gdn_scan",
    )(qkv, z, ba, conv_w, under_a(a_log), under_a(dt_bias), norm_g.reshape(1, C_DV), s0_t)


def _pad_cols(w, width):
    return jnp.pad(w, ((0, 0), (0, width - w.shape[1])))


def _pad_tokens(a, before, total):
    return jnp.pad(a, ((0, 0), (before, total - before - a.shape[1]), (0, 0)))


def _moba_mixer(x, g, w_in, layer, n_layers, kv_rows, cache=None):
    bsz, t, _ = x.shape
    x2d = x.reshape(bsz * t, D_MODEL)
    if cache is None:
        q, k, k_t, v_t = _norm_qkv_t(x2d, g, w_in.astype(BF16), t, layer, n_layers, kv_rows)
        o = _moba_prompt_attn(q.reshape(bsz, t, D_MODEL), k.reshape(bsz, t, D_MODEL), v_t, layer)
        return o.reshape(bsz * t, D_MODEL), (k_t, v_t)
    splits = tuple((i * D_MODEL, (i + 1) * D_MODEL) for i in range(3))
    if kv_rows[0] is None:
        kv_rows = tuple(jnp.zeros((n_layers, bsz * t, D_MODEL), F32) for _ in range(2))
    q, k, v = _norm_proj(x2d, g, w_in.astype(BF16), splits, stack=(layer, n_layers, kv_rows))
    k4 = k.reshape(n_layers, bsz, t, D_MODEL)
    v4 = v.reshape(n_layers, bsz, t, D_MODEL)
    o = _moba_sample_attn(q.reshape(bsz, t, D_MODEL), k4, v4, *cache, layer)
    return o.reshape(bsz * t, D_MODEL), (k, v)


def _gla_mixer(x, g, w_in, w_gate, b_gate, norm_g, s0):
    bsz, t, _ = x.shape
    nk = B_HEADS * B_DK
    edges = (0, nk, 2 * nk, 2 * nk + D_MODEL, 2 * nk + 2 * D_MODEL, 2 * nk + 2 * D_MODEL + LANES)
    splits = tuple(zip(edges[:-1], edges[1:]))
    parts = _norm_proj(x.reshape(bsz * t, D_MODEL), g, _pad_cols(w_in, edges[-1]).astype(BF16), splits)
    t_pad = -(-t // B_CHUNK) * B_CHUNK
    q, k, v, r, glr = (_pad_tokens(a.reshape(bsz, t, -1), 0, t_pad) for a in parts)
    wg = jnp.pad(w_gate, ((0, LANES - B_GATE_RANK), (0, 0))).astype(BF16)
    if s0 is None:
        s0_t = jnp.zeros((bsz, B_HEADS, B_DV, B_DK), F32)
    else:
        s0_t = jnp.swapaxes(s0, -1, -2)
    o, s_t = _gla_scan(q, k, v, r, glr, wg, b_gate, norm_g, s0_t, min(t, B_CHUNK))
    return o[:, :t].reshape(bsz * t, D_MODEL), jnp.swapaxes(s_t, -1, -2)


def _gdn_mixer(x, g, w_in, conv_w, a_log, dt_bias, norm_g, s0, conv_prev):
    bsz, t, _ = x.shape
    edges = (0, 3 * D_MODEL, 4 * D_MODEL, 4 * D_MODEL + LANES)
    splits = tuple(zip(edges[:-1], edges[1:]))
    parts = _norm_proj(x.reshape(bsz * t, D_MODEL), g, _pad_cols(w_in, edges[-1]).astype(BF16), splits)
    qkv, z, ba = (a.reshape(bsz, t, -1) for a in parts)
    hist = C_CONV - 1
    if conv_prev is None:
        lo = 0
        conv_new = qkv[:, t - hist:]
        s0_t = jnp.zeros((bsz, C_HEADS, C_DV, C_DK), F32)
        qkv_in = qkv
    else:
        lo = hist
        assert t >= hist
        conv_new = qkv[:, t - hist:]
        s0_t = jnp.swapaxes(s0, -1, -2)
        qkv_in = jnp.concatenate([conv_prev, qkv], axis=1)
    chunk = min(C_CHUNK, -(-(lo + t) // (2 * SUBLANES)) * 2 * SUBLANES)
    t_pad = -(-(lo + t) // chunk) * chunk
    if t_pad != lo + t or lo:
        assert t_pad == chunk
    qkv_in = _pad_tokens(qkv_in, 0, t_pad)
    z = _pad_tokens(z, lo, t_pad)
    ba = _pad_tokens(ba, lo, t_pad)
    o, s_t = _gdn_scan(qkv_in, z, ba, conv_w, a_log, dt_bias, norm_g, s0_t, chunk, lo,
                       min(lo + t, chunk))
    return o[:, lo:lo + t].reshape(bsz * t, D_MODEL), jnp.swapaxes(s_t, -1, -2), conv_new


def _run_group(x, cache, state_gla, state_gdn, state_gdn_conv, state_ffn_conv, w):
    bsz, t, _ = x.shape
    x2d = x.reshape(bsz * t, D_MODEL)
    out = {name: [] for name in ("gla", "gdn", "gconv", "fconv")}
    n_moba = w["moba_w_in"].shape[0]
    kv_rows = (None, None)
    for i in range(DEPTH):
        j = i // N_MIXERS
        xin = x2d.reshape(bsz, t, D_MODEL)
        if i % N_MIXERS == 0:
            o, kv_rows = _moba_mixer(xin, w["norm_mix"][i], w["moba_w_in"][j], j, n_moba, kv_rows, cache)
            w_o = w["moba_w_out"][j]
        elif i % N_MIXERS == 1:
            s0 = None if state_gla is None else state_gla[j]
            o, s = _gla_mixer(xin, w["norm_mix"][i], w["gla_w_in"][j], w["gla_w_gate"][j],
                              w["gla_b_gate"][j], w["gla_norm"][j], s0)
            out["gla"].append(s)
            w_o = w["gla_w_out"][j]
        else:
            s0 = None if state_gdn is None else state_gdn[j]
            cp = None if state_gdn_conv is None else state_gdn_conv[j]
            o, s, cn = _gdn_mixer(xin, w["norm_mix"][i], w["gdn_w_in"][j], w["gdn_conv_w"][j],
                                  w["gdn_a_log"][j], w["gdn_dt_bias"][j], w["gdn_norm"][j], s0, cp)
            out["gdn"].append(s)
            out["gconv"].append(cn)
            w_o = w["gdn_w_out"][j]
        prev = None if state_ffn_conv is None else state_ffn_conv[i]
        x2d, fc = _post(x2d, o, w_o.astype(BF16), w["norm_ffn"][i], w["ffn_w_in"][i].astype(BF16),
                        w["ffn_conv_w"][i], w["ffn_conv_b"][i], w["ffn_w_out"][i].astype(BF16),
                        w["norm_final"], i == DEPTH - 1, t, prev)
        out["fconv"].append(fc)
    y = x2d.reshape(bsz, t, D_MODEL)
    res = {name: jnp.stack(v) for name, v in out.items()}
    if cache is None:
        res["k"], res["v"] = (a.reshape(n_moba, bsz, A_HEADS, A_HEAD_DIM, t).transpose(0, 1, 4, 2, 3)
                              for a in kv_rows)
    else:
        rows = (n_moba, bsz, t, A_HEADS, A_HEAD_DIM)
        res["k"], res["v"] = kv_rows[0].reshape(rows), kv_rows[1].reshape(rows)
    return y, res


def kernel(x_prompt, x_sample, cache_k, cache_v, page_table, state_gla, state_gdn, state_gdn_conv,
           state_ffn_conv, norm_mix, norm_ffn, norm_final, moba_w_in, moba_w_out, gla_w_in, gla_w_gate,
           gla_b_gate, gla_norm, gla_w_out, gdn_w_in, gdn_conv_w, gdn_a_log, gdn_dt_bias, gdn_norm,
           gdn_w_out, ffn_w_in, ffn_conv_w, ffn_conv_b, ffn_w_out):
    w = dict(norm_mix=norm_mix, norm_ffn=norm_ffn, norm_final=norm_final, moba_w_in=moba_w_in,
             moba_w_out=moba_w_out, gla_w_in=gla_w_in, gla_w_gate=gla_w_gate, gla_b_gate=gla_b_gate,
             gla_norm=gla_norm, gla_w_out=gla_w_out, gdn_w_in=gdn_w_in, gdn_conv_w=gdn_conv_w,
             gdn_a_log=gdn_a_log, gdn_dt_bias=gdn_dt_bias, gdn_norm=gdn_norm, gdn_w_out=gdn_w_out,
             ffn_w_in=ffn_w_in, ffn_conv_w=ffn_conv_w, ffn_conv_b=ffn_conv_b, ffn_w_out=ffn_w_out)
    n_layers, n_pool = cache_k.shape[:2]
    pool_shape = (n_layers, n_pool, D_MODEL, PAGE_SIZE)
    cache = (cache_k.transpose(0, 1, 3, 4, 2).reshape(pool_shape),
             cache_v.transpose(0, 1, 3, 4, 2).reshape(pool_shape), page_table)
    yp, p = _run_group(x_prompt, None, None, None, None, None, w)
    ys, s = _run_group(x_sample, cache, state_gla, state_gdn, state_gdn_conv, state_ffn_conv, w)
    return (yp, ys, p["k"], p["v"], s["k"], s["v"], p["gla"], s["gla"], p["gdn"], s["gdn"],
            p["gconv"], s["gconv"], p["fconv"], s["fconv"])
```

```python
import functools
import math

import jax
import jax.numpy as jnp
from jax import lax
from jax.experimental import pallas as pl
from jax.experimental.pallas import tpu as pltpu

F32 = jnp.float32
BF16 = jnp.bfloat16

D_MODEL = 1024
DEPTH = 4
N_MIXERS = 3
NORM_EPS = 1e-6
NEG_INF = -1e30
PAGE_SIZE = 128

A_HEADS = 16
A_HEAD_DIM = D_MODEL // A_HEADS
MOBA_BLOCK = 256
MOBA_TOPK = 3

B_HEADS = 4
B_DK = D_MODEL // 2 // B_HEADS
B_DV = D_MODEL // B_HEADS
B_GATE_RANK = 16
B_GATE_TAU = 16.0
B_CHUNK = 64

C_HEADS = 8
C_DK = D_MODEL // C_HEADS
C_DV = D_MODEL // C_HEADS
C_CONV = 4
C_CHUNK = 64

D_FF = 2816
FFN_CONV = 3

LANES = 128
SUBLANES = 8
ROW_TILE = 512
FFN_COL_CHUNK = 2816
GLA_CHUNKS_PER_STEP = 8
SAMPLE_PAGES_PER_STEP = 16
VMEM_LIMIT = 56 * 1024 * 1024

NT_DIMS = (((1,), (1,)), ((), ()))
TN_DIMS = (((0,), (0,)), ((), ()))


def _params(sem, vmem=VMEM_LIMIT):
    return pltpu.CompilerParams(dimension_semantics=sem, vmem_limit_bytes=vmem)


def _const_spec(shape):
    nd = len(shape)
    return pl.BlockSpec(shape, lambda *_: (0,) * nd, pipeline_mode=pl.Buffered(1))


def _rms(x, g):
    return x * lax.rsqrt(jnp.mean(x * x, axis=-1, keepdims=True) + NORM_EPS) * g


def _sigmoid(x):
    return 1.0 / (1.0 + jnp.exp(-x))


def _silu(x):
    return x * _sigmoid(x)


def _softplus(x):
    return jnp.maximum(x, 0.0) + jnp.log(1.0 + jnp.exp(-jnp.abs(x)))


def _bdot(a, b):
    return jnp.dot(a.astype(BF16), b.astype(BF16), preferred_element_type=F32)


def _bdot_nt(a, b):
    return lax.dot_general(a.astype(BF16), b.astype(BF16), NT_DIMS, preferred_element_type=F32)


def _bdot_tn(a, b):
    return lax.dot_general(a.astype(BF16), b.astype(BF16), TN_DIMS, preferred_element_type=F32)


def _cumsum_rows(x):
    n = x.shape[0]
    row = lax.broadcasted_iota(jnp.int32, x.shape, 0)
    s = 1
    while s < n:
        x = x + jnp.where(row >= s, pltpu.roll(x, s, 0), 0.0)
        s *= 2
    return x


def _shift_rows(u, tail, s):
    r = pltpu.roll(u, s, 0)
    row = lax.broadcasted_iota(jnp.int32, tail.shape, 0)
    head = jnp.where(row < s, pltpu.roll(tail, s, 0), r[0:SUBLANES])
    return jnp.concatenate([head, r[SUBLANES:]], axis=0)


def _norm_proj_kernel(x_ref, g_ref, w_ref, *refs, splits):
    out_refs = refs[len(refs) - len(splits):]
    hb = _rms(x_ref[...], g_ref[...]).astype(BF16)
    for o_ref, (c0, c1) in zip(out_refs, splits):
        o_ref[...] = jnp.dot(hb, w_ref[:, c0:c1], preferred_element_type=F32).astype(o_ref.dtype)


def _norm_proj(x2d, g, w, splits, stack=None):
    n = x2d.shape[0]
    tm = min(ROW_TILE, n)
    out_shape = [jax.ShapeDtypeStruct((n, c1 - c0), F32) for c0, c1 in splits]
    out_specs = [pl.BlockSpec((tm, c1 - c0), lambda i: (i, 0)) for c0, c1 in splits]
    in_specs = [pl.BlockSpec((tm, D_MODEL), lambda i: (i, 0)), _const_spec((1, D_MODEL)),
                _const_spec(w.shape)]
    args = [x2d, g.reshape(1, D_MODEL), w]
    aliases = {}
    if stack is not None:
        layer, n_layers, buffers = stack
        first = len(splits) - len(buffers)
        for k, buf in enumerate(buffers):
            c0, c1 = splits[first + k]
            out_shape[first + k] = jax.ShapeDtypeStruct((n_layers, n, c1 - c0), F32)
            out_specs[first + k] = pl.BlockSpec((None, tm, c1 - c0), lambda i: (layer, i, 0))
            if buf is not None:
                aliases[len(args)] = first + k
                in_specs.append(pl.BlockSpec(memory_space=pl.ANY))
                args.append(buf)
    return pl.pallas_call(
        functools.partial(_norm_proj_kernel, splits=splits),
        grid=(n // tm,),
        in_specs=in_specs,
        out_specs=out_specs,
        out_shape=out_shape,
        input_output_aliases=aliases,
        compiler_params=_params(("parallel",)),
        name="norm_proj",
    )(*args)


def _norm_qkv_t_kernel(x_ref, g_ref, w_ref, *refs, layer, fresh):
    q_ref, k_ref, kt_ref, vt_ref, v_scr = refs[len(refs) - 5:]
    d = D_MODEL
    hb = _rms(x_ref[...], g_ref[...]).astype(BF16)
    q_ref[...] = jnp.dot(hb, w_ref[:, 0:d], preferred_element_type=F32)
    k_ref[...] = jnp.dot(hb, w_ref[:, d:2 * d], preferred_element_type=F32)
    v_scr[...] = jnp.dot(hb, w_ref[:, 2 * d:3 * d], preferred_element_type=F32)
    if fresh:
        for slot in range(kt_ref.shape[0]):
            if slot == layer:
                kt_ref[slot] = k_ref[...].T
                vt_ref[slot] = v_scr[...].T
            else:
                kt_ref[slot] = jnp.zeros(kt_ref.shape[1:], F32)
                vt_ref[slot] = jnp.zeros(vt_ref.shape[1:], F32)
    else:
        kt_ref[...] = k_ref[...].T
        vt_ref[...] = v_scr[...].T


def _norm_qkv_t(x2d, g, w, seq_len, layer, n_layers, buffers):
    n = x2d.shape[0]
    tm = min(ROW_TILE, seq_len)
    assert seq_len % tm == 0 and tm % LANES == 0
    tps = seq_len // tm
    fresh = buffers[0] is None
    assert fresh == (buffers[1] is None)
    row_spec = pl.BlockSpec((tm, D_MODEL), lambda i: (i, 0))
    if fresh:
        t_spec = pl.BlockSpec((n_layers, None, D_MODEL, tm), lambda i: (0, i // tps, 0, i % tps))
    else:
        t_spec = pl.BlockSpec((None, None, D_MODEL, tm), lambda i: (layer, i // tps, 0, i % tps))
    t_shape = jax.ShapeDtypeStruct((n_layers, n // seq_len, D_MODEL, seq_len), F32)
    in_specs = [row_spec, _const_spec((1, D_MODEL)), _const_spec(w.shape)]
    args = [x2d, g.reshape(1, D_MODEL), w]
    aliases = {}
    if not fresh:
        for k, buf in enumerate(buffers):
            aliases[len(args)] = 2 + k
            in_specs.append(pl.BlockSpec(memory_space=pl.ANY))
            args.append(buf)
    return pl.pallas_call(
        functools.partial(_norm_qkv_t_kernel, layer=layer, fresh=fresh),
        grid=(n // tm,),
        in_specs=in_specs,
        out_specs=[row_spec, row_spec, t_spec, t_spec],
        out_shape=[jax.ShapeDtypeStruct((n, D_MODEL), F32)] * 2 + [t_shape] * 2,
        scratch_shapes=[pltpu.VMEM((tm, D_MODEL), F32)],
        input_output_aliases=aliases,
        compiler_params=_params(("parallel",)),
        name="norm_qkv_t",
    )(*args)


def _post_kernel(*refs, per_seq_prev, tiles_per_seq, final_norm):
    if per_seq_prev:
        (x_ref, o_ref, wo_ref, g_ref, win_ref, cw_ref, cb_ref, wout_ref, gf_ref, p1_ref, p2_ref,
         y_ref, u_ref) = refs
    else:
        (x_ref, o_ref, wo_ref, g_ref, win_ref, cw_ref, cb_ref, wout_ref, gf_ref,
         y_ref, tail_ref, carry_scr) = refs

        @pl.when(pl.program_id(0) % tiles_per_seq == 0)
        def _():
            carry_scr[...] = jnp.zeros_like(carry_scr)

    tm = x_ref.shape[0]
    x = x_ref[...] + jnp.dot(o_ref[...].astype(BF16), wo_ref[...], preferred_element_type=F32)
    hb = _rms(x, g_ref[...]).astype(BF16)
    if per_seq_prev:
        t_in_seq = lax.broadcasted_iota(jnp.int32, (tm, 1), 0) % SUBLANES
    acc = x
    for c in range(D_FF // FFN_COL_CHUNK):
        conv = []
        for off in (c * FFN_COL_CHUNK, D_FF + c * FFN_COL_CHUNK):
            cols = slice(off, off + FFN_COL_CHUNK)
            u = jnp.dot(hb, win_ref[:, cols], preferred_element_type=F32)
            if per_seq_prev:
                u1 = jnp.where(t_in_seq < 1, p1_ref[:, cols], pltpu.roll(u, 1, 0))
                u2 = jnp.where(t_in_seq < 2, p2_ref[:, cols], pltpu.roll(u, 2, 0))
                u_ref[:, cols] = u
            else:
                tail = carry_scr[:, cols]
                u1 = _shift_rows(u, tail, 1)
                u2 = _shift_rows(u, tail, 2)
                carry_scr[:, cols] = u[tm - SUBLANES:tm]
                tail_ref[:, cols] = u[tm - SUBLANES:tm]
            conv.append(u2 * cw_ref[0:1, cols] + u1 * cw_ref[1:2, cols] + u * cw_ref[2:3, cols]
                        + cb_ref[:, cols])
        act = (_silu(conv[0]) * conv[1]).astype(BF16)
        acc = acc + jnp.dot(act, wout_ref[c * FFN_COL_CHUNK:(c + 1) * FFN_COL_CHUNK, :],
                            preferred_element_type=F32)
    y_ref[...] = _rms(acc, gf_ref[...]) if final_norm else acc


def _post(x2d, o2d, w_o, g, w_in, conv_w, conv_b, w_out, g_final, final_norm, seq_len, prev=None):
    n = x2d.shape[0]
    tm = min(ROW_TILE, n)
    nseq = n // seq_len
    row_spec = pl.BlockSpec((tm, D_MODEL), lambda i: (i, 0))
    in_specs = [row_spec, row_spec,
                _const_spec((D_MODEL, D_MODEL)), _const_spec((1, D_MODEL)),
                _const_spec((D_MODEL, 2 * D_FF)), _const_spec((FFN_CONV, 2 * D_FF)),
                _const_spec((1, 2 * D_FF)), _const_spec((D_FF, D_MODEL)), _const_spec((1, D_MODEL))]
    args = [x2d, o2d, w_o, g.reshape(1, D_MODEL), w_in, conv_w, conv_b.reshape(1, 2 * D_FF), w_out,
            g_final.reshape(1, D_MODEL)]
    y_shape = jax.ShapeDtypeStruct((n, D_MODEL), F32)
    if prev is None:
        assert seq_len % tm == 0
        tps = seq_len // tm
        y, tail = pl.pallas_call(
            functools.partial(_post_kernel, per_seq_prev=False, tiles_per_seq=tps, final_norm=final_norm),
            grid=(n // tm,),
            in_specs=in_specs,
            out_specs=[row_spec, pl.BlockSpec((None, SUBLANES, 2 * D_FF), lambda i: (i // tps, 0, 0))],
            out_shape=[y_shape, jax.ShapeDtypeStruct((nseq, SUBLANES, 2 * D_FF), F32)],
            scratch_shapes=[pltpu.VMEM((SUBLANES, 2 * D_FF), F32)],
            compiler_params=_params(("arbitrary",)),
            name="post_ffn_prompt",
        )(*args)
        return y, tail[:, SUBLANES - (FFN_CONV - 1):]
    assert seq_len == SUBLANES and n == tm
    pad = ((0, 0), (0, SUBLANES - 1), (0, 0))
    p1 = jnp.pad(prev[:, 1:2], pad).reshape(n, 2 * D_FF)
    p2 = jnp.pad(prev, ((0, 0), (0, SUBLANES - 2), (0, 0))).reshape(n, 2 * D_FF)
    wide_spec = pl.BlockSpec((tm, 2 * D_FF), lambda i: (i, 0))
    y, u = pl.pallas_call(
        functools.partial(_post_kernel, per_seq_prev=True, tiles_per_seq=1, final_norm=final_norm),
        grid=(1,),
        in_specs=in_specs + [wide_spec, wide_spec],
        out_specs=[row_spec, wide_spec],
        out_shape=[y_shape, jax.ShapeDtypeStruct((n, 2 * D_FF), F32)],
        compiler_params=_params(("arbitrary",)),
        name="post_ffn_sample",
    )(*args, p1, p2)
    return y, u.reshape(nseq, seq_len, 2 * D_FF)[:, seq_len - (FFN_CONV - 1):]


def _topk_bias_t(gate_t, n_valid, n_rows):
    row = lax.broadcasted_iota(jnp.int32, gate_t.shape, 0)
    valid = row < n_valid
    out = []
    for n in range(n_rows):
        gn = gate_t[n:n + 1, :]
        beats = jnp.where(gate_t > gn, 1.0, jnp.where((gate_t == gn) & (row < n), 1.0, 0.0))
        rank = jnp.sum(jnp.where(valid, beats, 0.0), axis=0, keepdims=True)
        out.append(jnp.where(rank < MOBA_TOPK, 0.0, NEG_INF))
    return out


def _moba_prompt_tile(c, q_ref, o_ref, kb_scr, vt_scr, km_scr):
    blk = MOBA_BLOCK
    hd = A_HEAD_DIM
    q = q_ref[...]
    lane = lax.broadcasted_iota(jnp.int32, (1, LANES), 1)
    causal = (lax.broadcasted_iota(jnp.int32, (blk, blk), 0)
              <= lax.broadcasted_iota(jnp.int32, (blk, blk), 1))
    pair = range(2)
    qh = [jnp.where(lane // hd == hh, q, 0.0) for hh in pair]
    qs = [(x * (hd ** -0.5 * math.log2(math.e))).astype(BF16) for x in qh]
    s = [[lax.dot_general(kb_scr[n], qs[hh], NT_DIMS, preferred_element_type=F32) for n in range(c + 1)]
         for hh in pair]
    for hh in pair:
        s[hh][c] = jnp.where(causal, s[hh][c], NEG_INF)
        if c > MOBA_TOPK:
            gate_t = _dot3(km_scr[...], qh[hh], NT_DIMS)
            for n, b in enumerate(_topk_bias_t(gate_t, c, c)):
                s[hh][n] = s[hh][n] + b
    m = []
    for hh in pair:
        mv = s[hh][c]
        for n in range(c):
            mv = jnp.maximum(mv, s[hh][n])
        m.append(jnp.max(mv, axis=0, keepdims=True))
    o_t = []
    for hh in pair:
        lv = None
        acc = None
        for n in range(c + 1):
            p = jnp.exp2(s[hh][n] - m[hh])
            lv = p if lv is None else lv + p
            pv = jnp.dot(vt_scr[n, hh * hd:(hh + 1) * hd, :], p.astype(BF16),
                         preferred_element_type=F32)
            acc = pv if acc is None else acc + pv
        o_t.append(acc / jnp.sum(lv, axis=0, keepdims=True))
    o_ref[...] = jnp.concatenate(o_t, axis=0).T.astype(o_ref.dtype)


def _moba_prompt_kernel(q_ref, k_ref, vt_ref, o_ref, kb_scr, vt_scr, km_scr, *, nb):
    i = pl.program_id(2)
    blk = MOBA_BLOCK

    @pl.when(i == 0)
    def _():
        km_scr[...] = jnp.zeros_like(km_scr)
        for n in range(nb):
            kn = k_ref[n * blk:(n + 1) * blk, :]
            kb_scr[n] = kn.astype(BF16)
            vt_scr[n] = vt_ref[:, n * blk:(n + 1) * blk].astype(BF16)
            km_scr[n:n + 1, :] = jnp.mean(kn, axis=0, keepdims=True)

    for c in range(nb):
        pl.when(i == c)(functools.partial(_moba_prompt_tile, c, q_ref, o_ref, kb_scr, vt_scr, km_scr))


def _moba_prompt_attn(q, k, v_t, layer):
    bsz, s, _ = q.shape
    assert s % MOBA_BLOCK == 0
    nb = s // MOBA_BLOCK
    nb_pad = -(-nb // (2 * SUBLANES)) * 2 * SUBLANES
    hp = D_MODEL // LANES
    q_spec = pl.BlockSpec((None, MOBA_BLOCK, LANES), lambda b, h, i: (b, i, h))
    k_spec = pl.BlockSpec((None, s, LANES), lambda b, h, i: (b, 0, h))
    vt_spec = pl.BlockSpec((None, None, LANES, s), lambda b, h, i: (layer, b, h, 0))
    return pl.pallas_call(
        functools.partial(_moba_prompt_kernel, nb=nb),
        grid=(bsz, hp, nb),
        in_specs=[q_spec, k_spec, vt_spec],
        out_specs=q_spec,
        out_shape=jax.ShapeDtypeStruct((bsz, s, D_MODEL), BF16),
        scratch_shapes=[pltpu.VMEM((nb, MOBA_BLOCK, LANES), BF16),
                        pltpu.VMEM((nb, LANES, MOBA_BLOCK), BF16),
                        pltpu.VMEM((nb_pad, LANES), F32)],
        compiler_params=_params(("parallel", "parallel", "arbitrary")),
        name="moba_prompt_attn",
    )(q, k, v_t)


def _topk_select(gate, n_valid, n_cols):
    col = lax.broadcasted_iota(jnp.int32, gate.shape, 1)
    valid = col < n_valid
    sel = []
    for n in range(n_cols):
        gn = gate[:, n:n + 1]
        beats = jnp.where(gate > gn, 1.0, jnp.where((gate == gn) & (col < n), 1.0, 0.0))
        rank = jnp.sum(jnp.where(valid, beats, 0.0), axis=1, keepdims=True)
        sel.append(rank < MOBA_TOPK)
    return sel


def _moba_sample_kernel(pt_ref, q_ref, kn_ref, vn_ref, *refs, n_pages, t_new):
    g_pages = SAMPLE_PAGES_PER_STEP
    k_refs = refs[:g_pages]
    v_refs = refs[g_pages:2 * g_pages]
    o_ref = refs[2 * g_pages]
    qbf_scr, new_scr, gate_scr, sc_scr, l_scr, acc_scr = refs[2 * g_pages + 1:]
    del pt_ref
    j = pl.program_id(1)
    n_steps = n_pages // g_pages
    n_blocks = n_pages * PAGE_SIZE // MOBA_BLOCK
    pages_per_block = MOBA_BLOCK // PAGE_SIZE
    blocks_per_step = g_pages // pages_per_block
    rows = A_HEADS * t_new
    head_of_row = lax.broadcasted_iota(jnp.int32, (rows, D_MODEL), 0) // t_new
    head_of_lane = lax.broadcasted_iota(jnp.int32, (rows, D_MODEL), 1) // A_HEAD_DIM

    @pl.when(j == 0)
    def _():
        qbd = jnp.where(head_of_row == head_of_lane,
                        jnp.concatenate([q_ref[...]] * A_HEADS, axis=0), 0.0)
        qbf_scr[...] = (qbd * (A_HEAD_DIM ** -0.5)).astype(BF16)
        gate_scr[...] = jnp.zeros_like(gate_scr)

    @pl.when(j < n_steps)
    def _():
        col = lax.broadcasted_iota(jnp.int32, gate_scr.shape, 1)
        gate = gate_scr[...]
        for bb in range(blocks_per_step):
            ssum = None
            for g in range(bb * pages_per_block, (bb + 1) * pages_per_block):
                s = jnp.dot(qbf_scr[...], k_refs[g][...].astype(BF16), preferred_element_type=F32)
                sc_scr[j * g_pages + g] = s
                ssum = s if ssum is None else ssum + s
            gmean = jnp.sum(ssum, axis=1, keepdims=True) * (1.0 / MOBA_BLOCK)
            gate = jnp.where(col == j * blocks_per_step + bb, gmean, gate)
        gate_scr[...] = gate

    @pl.when(j == n_steps - 1)
    def _():
        sel = _topk_select(gate_scr[...], n_blocks, n_blocks)
        new_scr[...] = jnp.zeros_like(new_scr)
        new_scr[0, 0:t_new, :] = kn_ref[...].astype(BF16)
        new_scr[1, 0:t_new, :] = vn_ref[...].astype(BF16)
        s_own = lax.dot_general(qbf_scr[...], new_scr[0], NT_DIMS, preferred_element_type=F32)
        r_tok = lax.broadcasted_iota(jnp.int32, s_own.shape, 0) % t_new
        c_tok = lax.broadcasted_iota(jnp.int32, s_own.shape, 1)
        s_own = jnp.where(c_tok <= r_tok, s_own, NEG_INF)
        mv = s_own
        for b in range(n_blocks):
            bias = jnp.broadcast_to(jnp.where(sel[b], 0.0, NEG_INF), s_own.shape)
            for p in range(b * pages_per_block, (b + 1) * pages_per_block):
                mv = jnp.maximum(mv, sc_scr[p] + bias)
        m = jnp.max(mv, axis=1, keepdims=True)
        p_own = jnp.exp(s_own - m)
        lv = p_own
        for b in range(n_blocks):
            shift = jnp.broadcast_to(jnp.where(sel[b], 0.0, NEG_INF) - m, s_own.shape)
            for p in range(b * pages_per_block, (b + 1) * pages_per_block):
                e = jnp.exp(sc_scr[p] + shift)
                lv = lv + e
                sc_scr[p] = e
        l_scr[...] = jnp.sum(lv, axis=1, keepdims=True)
        acc_scr[...] = jnp.dot(p_own.astype(BF16), new_scr[1], preferred_element_type=F32)

    @pl.when(j >= n_steps)
    def _():
        acc = acc_scr[...]
        for g in range(g_pages):
            pp = sc_scr[(j - n_steps) * g_pages + g].astype(BF16)
            acc = acc + lax.dot_general(pp, v_refs[g][...].astype(BF16), NT_DIMS,
                                        preferred_element_type=F32)
        acc_scr[...] = acc

    @pl.when(j == 2 * n_steps - 1)
    def _():
        acc = jnp.where(head_of_row == head_of_lane, acc_scr[...] / l_scr[...], 0.0)
        o = acc[0:t_new]
        for h in range(1, A_HEADS):
            o = o + acc[h * t_new:(h + 1) * t_new]
        o_ref[...] = o


def _moba_sample_attn(q, k_new, v_new, cache_kt, cache_vt, page_table, layer):
    bsz, t_new, _ = q.shape
    n_pages = page_table.shape[1]
    g_pages = SAMPLE_PAGES_PER_STEP
    assert (n_pages * PAGE_SIZE) % MOBA_BLOCK == 0 and MOBA_BLOCK % PAGE_SIZE == 0
    assert (g_pages * PAGE_SIZE) % MOBA_BLOCK == 0 and n_pages % g_pages == 0
    assert t_new == SUBLANES and A_HEADS * t_new == LANES and PAGE_SIZE == LANES
    n_steps = n_pages // g_pages
    n_blocks = n_pages * PAGE_SIZE // MOBA_BLOCK
    assert MOBA_TOPK <= n_blocks <= LANES
    tok_spec = pl.BlockSpec((None, t_new, D_MODEL), lambda b, j, pt: (b, 0, 0))
    new_spec = pl.BlockSpec((None, None, t_new, D_MODEL), lambda b, j, pt: (layer, b, 0, 0))

    def k_spec(g):
        return pl.BlockSpec(
            (None, None, D_MODEL, PAGE_SIZE),
            lambda b, j, pt: (layer, pt[b, g_pages * jnp.minimum(j, n_steps - 1) + g], 0, 0))

    def v_spec(g):
        def index(b, j, pt):
            in_v = j >= n_steps
            seq = jnp.where(in_v, b, jnp.maximum(b - 1, 0))
            step = jnp.where(in_v, j - n_steps, n_steps - 1)
            return (layer, pt[seq, g_pages * step + g], 0, 0)
        return pl.BlockSpec((None, None, D_MODEL, PAGE_SIZE), index)

    rows = A_HEADS * t_new
    grid_spec = pltpu.PrefetchScalarGridSpec(
        num_scalar_prefetch=1,
        grid=(bsz, 2 * n_steps),
        in_specs=[tok_spec, new_spec, new_spec] + [k_spec(g) for g in range(g_pages)]
        + [v_spec(g) for g in range(g_pages)],
        out_specs=tok_spec,
        scratch_shapes=[pltpu.VMEM((rows, D_MODEL), BF16),
                        pltpu.VMEM((2, LANES, D_MODEL), BF16), pltpu.VMEM((rows, LANES), F32),
                        pltpu.VMEM((n_pages, rows, PAGE_SIZE), F32), pltpu.VMEM((rows, 1), F32),
                        pltpu.VMEM((rows, D_MODEL), F32)],
    )
    return pl.pallas_call(
        functools.partial(_moba_sample_kernel, n_pages=n_pages, t_new=t_new),
        grid_spec=grid_spec,
        out_shape=jax.ShapeDtypeStruct((bsz, t_new, D_MODEL), F32),
        compiler_params=_params(("parallel", "arbitrary")),
        name="moba_sample_attn",
    )(page_table, q, k_new, v_new, *([cache_kt] * g_pages), *([cache_vt] * g_pages))


def _gla_kernel(q_ref, k_ref, v_ref, r_ref, glr_ref, wg_ref, bg_ref, ng_ref, s0_ref,
                o_ref, sout_ref, s_scr, *, c, valid_rows):
    ci = pl.program_id(1)
    n_sub = q_ref.shape[0] // c

    @pl.when(ci == 0)
    def _():
        s_scr[...] = s0_ref[...]

    causal = (lax.broadcasted_iota(jnp.int32, (c, c), 1) <= lax.broadcasted_iota(jnp.int32, (c, c), 0))
    st = [s_scr[h] for h in range(B_HEADS)]
    for j in range(n_sub):
        rows = slice(j * c, (j + 1) * c)
        x = (jnp.dot(glr_ref[rows, :].astype(BF16), wg_ref[...], preferred_element_type=F32)
             + bg_ref[...])
        gk = -_softplus(-x) * (1.0 / B_GATE_TAU)
        if valid_rows < c:
            gk = jnp.where(lax.broadcasted_iota(jnp.int32, gk.shape, 0) < valid_rows, gk, 0.0)
        b = _cumsum_rows(gk)
        b_last = b[c - 1:c, :]
        k = k_ref[rows, :]
        q_t = (q_ref[rows, :] * (B_DK ** -0.5) * jnp.exp(b)).astype(BF16)
        k_t = (k * jnp.exp(-b)).astype(BF16)
        k_end = (k * jnp.exp(b_last - b)).astype(BF16)
        e_last = jnp.exp(b_last)
        for h in range(B_HEADS):
            kc = slice(h * B_DK, (h + 1) * B_DK)
            vc = slice(h * B_DV, (h + 1) * B_DV)
            vh = v_ref[rows, vc].astype(BF16)
            att = jnp.where(causal, lax.dot_general(q_t[:, kc], k_t[:, kc], NT_DIMS,
                                                    preferred_element_type=F32), 0.0)
            o = (jnp.dot(att.astype(BF16), vh, preferred_element_type=F32)
                 + lax.dot_general(q_t[:, kc], st[h].astype(BF16), NT_DIMS,
                                   preferred_element_type=F32))
            st[h] = st[h] * e_last[:, kc] + lax.dot_general(vh, k_end[:, kc], TN_DIMS,
                                                            preferred_element_type=F32)
            o = o * lax.rsqrt(jnp.mean(o * o, axis=-1, keepdims=True) + NORM_EPS) * ng_ref[...]
            o_ref[rows, vc] = (o * _silu(r_ref[rows, vc])).astype(o_ref.dtype)
    for h in range(B_HEADS):
        s_scr[h] = st[h]

    @pl.when(ci == pl.num_programs(1) - 1)
    def _():
        sout_ref[...] = s_scr[...]


def _gla_scan(q, k, v, r, glr, w_gate, b_gate, norm_g, s0_t, c, valid_rows):
    bsz, t, _ = q.shape
    assert t % c == 0
    rows = math.gcd(t, GLA_CHUNKS_PER_STEP * c)
    assert valid_rows == c or rows == c
    nk = B_HEADS * B_DK

    def tok(width):
        return pl.BlockSpec((None, rows, width), lambda b, ci: (b, ci, 0))

    st_spec = pl.BlockSpec((None, B_HEADS, B_DV, B_DK), lambda b, ci: (b, 0, 0, 0))
    return pl.pallas_call(
        functools.partial(_gla_kernel, c=c, valid_rows=valid_rows),
        grid=(bsz, t // rows),
        in_specs=[tok(nk), tok(nk), tok(D_MODEL), tok(D_MODEL), tok(LANES),
                  _const_spec((LANES, nk)), _const_spec((1, nk)), _const_spec((1, B_DV)), st_spec],
        out_specs=[tok(D_MODEL), st_spec],
        out_shape=[jax.ShapeDtypeStruct((bsz, t, D_MODEL), BF16),
                   jax.ShapeDtypeStruct((bsz, B_HEADS, B_DV, B_DK), F32)],
        scratch_shapes=[pltpu.VMEM((B_HEADS, B_DV, B_DK), F32)],
        compiler_params=_params(("parallel", "arbitrary")),
        name="gla_scan",
    )(q, k, v, r, glr, w_gate, b_gate.reshape(1, nk), norm_g.reshape(1, B_DV), s0_t)


def _split_bf16(x):
    hi = x.astype(BF16)
    return hi, (x - hi.astype(F32)).astype(BF16)


def _dot3(a, b, dims=(((1,), (0,)), ((), ()))):
    ah, al = _split_bf16(a)
    bh, bl = _split_bf16(b)

    def f(x, y):
        return lax.dot_general(x, y, dims, preferred_element_type=F32)
    return f(ah, bh) + (f(ah, bl) + f(al, bh))


def _unit_lower_inverse_minus_eye(a_list):
    c = a_list[0].shape[0]
    base = SUBLANES
    assert c % base == 0 and (c // base) & (c // base - 1) == 0
    ri = lax.broadcasted_iota(jnp.int32, (c, c), 0)
    cj = lax.broadcasted_iota(jnp.int32, (c, c), 1)
    x = [jnp.where(ri // base == cj // base, -a, 0.0) for a in a_list]
    n = list(x)
    p = 2
    while p < base:
        x = [_dot3(xi, xi) for xi in x]
        n = [ni + xi + _dot3(ni, xi) for ni, xi in zip(n, x)]
        p *= 2
    m = base
    while m < c:
        pair = (ri // (2 * m) == cj // (2 * m)) & (ri // m != cj // m)
        low = [jnp.where(pair, a, 0.0) for a in a_list]
        d_l = [li + _dot3(ni, li) for ni, li in zip(n, low)]
        n = [ni - dli - _dot3(dli, ni) for ni, dli in zip(n, d_l)]
        m *= 2
    return n


def _gdn_kernel(qkv_ref, z_ref, ba_ref, cw_ref, alog_ref, dtb_ref, ng_ref, s0_ref,
                o_ref, sout_ref, s_scr, carry_scr, *, valid_lo, valid_hi):
    ci = pl.program_id(1)
    c = qkv_ref.shape[0]
    heads = range(C_HEADS)

    @pl.when(ci == 0)
    def _():
        s_scr[...] = s0_ref[...]
        carry_scr[0:SUBLANES, :] = jnp.zeros((SUBLANES, carry_scr.shape[1]), F32)

    u = qkv_ref[...]
    carry_scr[SUBLANES:SUBLANES + c, :] = u
    conv = u * cw_ref[C_CONV - 1:C_CONV, :]
    for s in range(1, C_CONV):
        conv = conv + carry_scr[SUBLANES - s:SUBLANES - s + c, :] * cw_ref[C_CONV - 1 - s:C_CONV - s, :]
    carry_scr[0:SUBLANES, :] = u[c - SUBLANES:c]
    xc = _silu(conv)

    ba = ba_ref[...]
    beta_all = _sigmoid(ba)
    g_all = -jnp.exp(alog_ref[...]) * _softplus(ba + dtb_ref[...])
    if valid_lo > 0 or valid_hi < c:
        rowi = lax.broadcasted_iota(jnp.int32, ba.shape, 0)
        ok = (rowi >= valid_lo) & (rowi < valid_hi)
        beta_all = jnp.where(ok, beta_all, 0.0)
        g_all = jnp.where(ok, g_all, 0.0)
    gam_all = _cumsum_rows(g_all)
    gam_t = jnp.concatenate([gam_all, jnp.zeros((LANES - c, LANES), F32)], axis=0).T
    ri = lax.broadcasted_iota(jnp.int32, (c, c), 0)
    cj = lax.broadcasted_iota(jnp.int32, (c, c), 1)
    incl = cj <= ri
    strict = cj < ri

    qn, kn, gcol, dec, rhs, a_mat = [], [], [], [], [], []
    for h in heads:
        qh = xc[:, h * C_DK:(h + 1) * C_DK]
        kh = xc[:, D_MODEL + h * C_DK:D_MODEL + (h + 1) * C_DK]
        vh = xc[:, 2 * D_MODEL + h * C_DV:2 * D_MODEL + (h + 1) * C_DV]
        qn.append(qh * lax.rsqrt(jnp.sum(qh * qh, axis=-1, keepdims=True) + NORM_EPS) * (C_DK ** -0.5))
        kn.append(kh * lax.rsqrt(jnp.sum(kh * kh, axis=-1, keepdims=True) + NORM_EPS))
        beta = beta_all[:, h:h + 1]
        gcol.append(gam_all[:, C_HEADS + h:C_HEADS + h + 1])
        grow = gam_t[C_HEADS + h:C_HEADS + h + 1, 0:c]
        dec.append(jnp.where(incl, jnp.exp(jnp.where(incl, gcol[h] - grow, 0.0)), 0.0))
        kb = kn[h] * beta
        rhs.append(jnp.concatenate([kb * jnp.exp(gcol[h]), vh * beta], axis=1))
        a_mat.append(jnp.where(strict, _dot3(kb, kn[h], NT_DIMS) * dec[h], 0.0))
    inv_m = _unit_lower_inverse_minus_eye(a_mat)
    wu = [rhs[h] + _dot3(inv_m[h], rhs[h]) for h in heads]
    att = [_bdot_nt(qn[h], kn[h]) * dec[h] for h in heads]

    st = [s_scr[h] for h in heads]
    stb = [s.astype(BF16) for s in st]
    v_new = [wu[h][:, C_DK:] - lax.dot_general(wu[h][:, :C_DK].astype(BF16), stb[h], NT_DIMS,
                                               preferred_element_type=F32) for h in heads]
    o = [lax.dot_general((qn[h] * jnp.exp(gcol[h])).astype(BF16), stb[h], NT_DIMS,
                         preferred_element_type=F32) + _bdot(att[h], v_new[h]) for h in heads]
    for h in heads:
        g_last = gcol[h][c - 1:c, :]
        s_scr[h] = jnp.exp(g_last) * st[h] + _bdot_tn(v_new[h], kn[h] * jnp.exp(g_last - gcol[h]))
    for h in heads:
        hc = slice(h * C_DV, (h + 1) * C_DV)
        oh = o[h] * lax.rsqrt(jnp.mean(o[h] * o[h], axis=-1, keepdims=True) + NORM_EPS) * ng_ref[...]
        o_ref[:, hc] = (oh * _silu(z_ref[:, hc])).astype(o_ref.dtype)

    @pl.when(ci == pl.num_programs(1) - 1)
    def _():
        sout_ref[...] = s_scr[...]


def _gdn_scan(qkv, z, ba, conv_w, a_log, dt_bias, norm_g, s0_t, c, valid_lo, valid_hi):
    bsz, t, _ = qkv.shape
    assert t % c == 0

    def tok(width):
        return pl.BlockSpec((None, c, width), lambda b, ci: (b, ci, 0))

    def under_a(p):
        return jnp.zeros((1, LANES), F32).at[0, C_HEADS:2 * C_HEADS].set(p)

    st_spec = pl.BlockSpec((None, C_HEADS, C_DV, C_DK), lambda b, ci: (b, 0, 0, 0))
    return pl.pallas_call(
        functools.partial(_gdn_kernel, valid_lo=valid_lo, valid_hi=valid_hi),
        grid=(bsz, t // c),
        in_specs=[tok(3 * D_MODEL), tok(D_MODEL), tok(LANES),
                  _const_spec((C_CONV, 3 * D_MODEL)), _const_spec((1, LANES)),
                  _const_spec((1, LANES)), _const_spec((1, C_DV)), st_spec],
        out_specs=[tok(D_MODEL), st_spec],
        out_shape=[jax.ShapeDtypeStruct((bsz, t, D_MODEL), BF16),
                   jax.ShapeDtypeStruct((bsz, C_HEADS, C_DV, C_DK), F32)],
        scratch_shapes=[pltpu.VMEM((C_HEADS, C_DV, C_DK), F32),
                        pltpu.VMEM((SUBLANES + c, 3 * D_MODEL), F32)],
        compiler_params=_params(("parallel", "arbitrary")),
        name="gdn_scan",
    )(qkv, z, ba, conv_w, under_a(a_log), under_a(dt_bias), norm_g.reshape(1, C_DV), s0_t)


def _pad_cols(w, width):
    return jnp.pad(w, ((0, 0), (0, width - w.shape[1])))


def _pad_tokens(a, before, total):
    return jnp.pad(a, ((0, 0), (before, total - before - a.shape[1]), (0, 0)))


def _moba_mixer(x, g, w_in, layer, n_layers, kv_rows, cache=None):
    bsz, t, _ = x.shape
    x2d = x.reshape(bsz * t, D_MODEL)
    if cache is None:
        q, k, k_t, v_t = _norm_qkv_t(x2d, g, w_in.astype(BF16), t, layer, n_layers, kv_rows)
        o = _moba_prompt_attn(q.reshape(bsz, t, D_MODEL), k.reshape(bsz, t, D_MODEL), v_t, layer)
        return o.reshape(bsz * t, D_MODEL), (k_t, v_t)
    splits = tuple((i * D_MODEL, (i + 1) * D_MODEL) for i in range(3))
    if kv_rows[0] is None:
        kv_rows = tuple(jnp.zeros((n_layers, bsz * t, D_MODEL), F32) for _ in range(2))
    q, k, v = _norm_proj(x2d, g, w_in.astype(BF16), splits, stack=(layer, n_layers, kv_rows))
    k4 = k.reshape(n_layers, bsz, t, D_MODEL)
    v4 = v.reshape(n_layers, bsz, t, D_MODEL)
    o = _moba_sample_attn(q.reshape(bsz, t, D_MODEL), k4, v4, *cache, layer)
    return o.reshape(bsz * t, D_MODEL), (k, v)


def _gla_mixer(x, g, w_in, w_gate, b_gate, norm_g, s0):
    bsz, t, _ = x.shape
    nk = B_HEADS * B_DK
    edges = (0, nk, 2 * nk, 2 * nk + D_MODEL, 2 * nk + 2 * D_MODEL, 2 * nk + 2 * D_MODEL + LANES)
    splits = tuple(zip(edges[:-1], edges[1:]))
    parts = _norm_proj(x.reshape(bsz * t, D_MODEL), g, _pad_cols(w_in, edges[-1]).astype(BF16), splits)
    chunk = min(B_CHUNK, -(-t // (2 * SUBLANES)) * 2 * SUBLANES)
    t_pad = -(-t // chunk) * chunk
    q, k, v, r, glr = (_pad_tokens(a.reshape(bsz, t, -1), 0, t_pad) for a in parts)
    wg = jnp.pad(w_gate, ((0, LANES - B_GATE_RANK), (0, 0))).astype(BF16)
    if s0 is None:
        s0_t = jnp.zeros((bsz, B_HEADS, B_DV, B_DK), F32)
    else:
        s0_t = jnp.swapaxes(s0, -1, -2)
    o, s_t = _gla_scan(q, k, v, r, glr, wg, b_gate, norm_g, s0_t, chunk, min(t, chunk))
    return o[:, :t].reshape(bsz * t, D_MODEL), jnp.swapaxes(s_t, -1, -2)


def _gdn_mixer(x, g, w_in, conv_w, a_log, dt_bias, norm_g, s0, conv_prev):
    bsz, t, _ = x.shape
    edges = (0, 3 * D_MODEL, 4 * D_MODEL, 4 * D_MODEL + LANES)
    splits = tuple(zip(edges[:-1], edges[1:]))
    parts = _norm_proj(x.reshape(bsz * t, D_MODEL), g, _pad_cols(w_in, edges[-1]).astype(BF16), splits)
    qkv, z, ba = (a.reshape(bsz, t, -1) for a in parts)
    hist = C_CONV - 1
    if conv_prev is None:
        lo = 0
        conv_new = qkv[:, t - hist:]
        s0_t = jnp.zeros((bsz, C_HEADS, C_DV, C_DK), F32)
        qkv_in = qkv
    else:
        lo = hist
        assert t >= hist
        conv_new = qkv[:, t - hist:]
        s0_t = jnp.swapaxes(s0, -1, -2)
        qkv_in = jnp.concatenate([conv_prev, qkv], axis=1)
    chunk = min(C_CHUNK, -(-(lo + t) // (2 * SUBLANES)) * 2 * SUBLANES)
    t_pad = -(-(lo + t) // chunk) * chunk
    if t_pad != lo + t or lo:
        assert t_pad == chunk
    qkv_in = _pad_tokens(qkv_in, 0, t_pad)
    z = _pad_tokens(z, lo, t_pad)
    ba = _pad_tokens(ba, lo, t_pad)
    o, s_t = _gdn_scan(qkv_in, z, ba, conv_w, a_log, dt_bias, norm_g, s0_t, chunk, lo,
                       min(lo + t, chunk))
    return o[:, lo:lo + t].reshape(bsz * t, D_MODEL), jnp.swapaxes(s_t, -1, -2), conv_new


def _run_group(x, cache, state_gla, state_gdn, state_gdn_conv, state_ffn_conv, w):
    bsz, t, _ = x.shape
    x2d = x.reshape(bsz * t, D_MODEL)
    out = {name: [] for name in ("gla", "gdn", "gconv", "fconv")}
    n_moba = w["moba_w_in"].shape[0]
    kv_rows = (None, None)
    for i in range(DEPTH):
        j = i // N_MIXERS
        xin = x2d.reshape(bsz, t, D_MODEL)
        if i % N_MIXERS == 0:
            o, kv_rows = _moba_mixer(xin, w["norm_mix"][i], w["moba_w_in"][j], j, n_moba, kv_rows, cache)
            w_o = w["moba_w_out"][j]
        elif i % N_MIXERS == 1:
            s0 = None if state_gla is None else state_gla[j]
            o, s = _gla_mixer(xin, w["norm_mix"][i], w["gla_w_in"][j], w["gla_w_gate"][j],
                              w["gla_b_gate"][j], w["gla_norm"][j], s0)
            out["gla"].append(s)
            w_o = w["gla_w_out"][j]
        else:
            s0 = None if state_gdn is None else state_gdn[j]
            cp = None if state_gdn_conv is None else state_gdn_conv[j]
            o, s, cn = _gdn_mixer(xin, w["norm_mix"][i], w["gdn_w_in"][j], w["gdn_conv_w"][j],
                                  w["gdn_a_log"][j], w["gdn_dt_bias"][j], w["gdn_norm"][j], s0, cp)
            out["gdn"].append(s)
            out["gconv"].append(cn)
            w_o = w["gdn_w_out"][j]
        prev = None if state_ffn_conv is None else state_ffn_conv[i]
        x2d, fc = _post(x2d, o, w_o.astype(BF16), w["norm_ffn"][i], w["ffn_w_in"][i].astype(BF16),
                        w["ffn_conv_w"][i], w["ffn_conv_b"][i], w["ffn_w_out"][i].astype(BF16),
                        w["norm_final"], i == DEPTH - 1, t, prev)
        out["fconv"].append(fc)
    y = x2d.reshape(bsz, t, D_MODEL)
    res = {name: jnp.stack(v) for name, v in out.items()}
    if cache is None:
        res["k"], res["v"] = (a.reshape(n_moba, bsz, A_HEADS, A_HEAD_DIM, t).transpose(0, 1, 4, 2, 3)
                              for a in kv_rows)
    else:
        rows = (n_moba, bsz, t, A_HEADS, A_HEAD_DIM)
        res["k"], res["v"] = kv_rows[0].reshape(rows), kv_rows[1].reshape(rows)
    return y, res


def kernel(x_prompt, x_sample, cache_k, cache_v, page_table, state_gla, state_gdn, state_gdn_conv,
           state_ffn_conv, norm_mix, norm_ffn, norm_final, moba_w_in, moba_w_out, gla_w_in, gla_w_gate,
           gla_b_gate, gla_norm, gla_w_out, gdn_w_in, gdn_conv_w, gdn_a_log, gdn_dt_bias, gdn_norm,
           gdn_w_out, ffn_w_in, ffn_conv_w, ffn_conv_b, ffn_w_out):
    w = dict(norm_mix=norm_mix, norm_ffn=norm_ffn, norm_final=norm_final, moba_w_in=moba_w_in,
             moba_w_out=moba_w_out, gla_w_in=gla_w_in, gla_w_gate=gla_w_gate, gla_b_gate=gla_b_gate,
             gla_norm=gla_norm, gla_w_out=gla_w_out, gdn_w_in=gdn_w_in, gdn_conv_w=gdn_conv_w,
             gdn_a_log=gdn_a_log, gdn_dt_bias=gdn_dt_bias, gdn_norm=gdn_norm, gdn_w_out=gdn_w_out,
             ffn_w_in=ffn_w_in, ffn_conv_w=ffn_conv_w, ffn_conv_b=ffn_conv_b, ffn_w_out=ffn_w_out)
    n_layers, n_pool = cache_k.shape[:2]
    pool_shape = (n_layers, n_pool, D_MODEL, PAGE_SIZE)
    cache = (cache_k.transpose(0, 1, 3, 4, 2).reshape(pool_shape),
             cache_v.transpose(0, 1, 3, 4, 2).reshape(pool_shape), page_table)
    yp, p = _run_group(x_prompt, None, None, None, None, None, w)
    ys, s = _run_group(x_sample, cache, state_gla, state_gdn, state_gdn_conv, state_ffn_conv, w)
    return (yp, ys, p["k"], p["v"], s["k"], s["v"], p["gla"], s["gla"], p["gdn"], s["gdn"],
            p["gconv"], s["gconv"], p["fconv"], s["fconv"])
```
